```python
import jax, jax.numpy as jnp
from jax import lax
import numpy as np


D_MODEL = 1024
BATCH = 32
SEQ = 2048
DEPTH = 2
DEC_BATCH = 16
DEC_SEQ = 4096
PAST_LEN = 128

A_HEADS = 4
A_HEAD_DIM = 128
A_WIDTH = A_HEADS * A_HEAD_DIM
A_CHUNK = 64
B_GROUPS = ((128, 1), (512, 4), (2048, 16))
B_HEADS_PER_GROUP = 4
B_HEADS = B_HEADS_PER_GROUP * len(B_GROUPS)
B_HEAD_DIM = 64
B_WIDTH = B_HEADS * B_HEAD_DIM
B_QBLOCK = 64
C_WIDTH = D_MODEL
CONV_WIDTH = 31
D_FF = 4 * D_MODEL
REL_BUCKETS = 32
REL_MAX_DIST = 1024
AB_IN = 5 * A_WIDTH + 3 * B_WIDTH
AB_OUT = A_WIDTH + B_WIDTH
N_AB = (DEPTH + 1) // 2
N_C = DEPTH // 2
ALPHA = (2 * DEPTH) ** 0.25
BETA = (8 * DEPTH) ** -0.25
LN_EPS = 1e-5
RMS_EPS = 1e-6
NEG_INF = -1e30

kernel_name = 'hybrid_hgrn2_dilated_conformer_encoder'


def layer_norm(x, g, b):
    xf = x.astype(jnp.float32)
    mu = jnp.mean(xf, -1, keepdims=True)
    var = jnp.mean(jnp.square(xf - mu), -1, keepdims=True)
    y = (xf - mu) * lax.rsqrt(var + LN_EPS) * g.astype(jnp.float32) + b.astype(jnp.float32)
    return y.astype(x.dtype)


def t5_buckets(rel):
    half = REL_BUCKETS // 2
    max_exact = half // 2
    n = np.abs(rel)
    large = max_exact + (np.log(np.maximum(n, 1) / max_exact) / np.log(REL_MAX_DIST / max_exact)
                         * (half - max_exact)).astype(np.int32)
    large = np.minimum(large, half - 1)
    return (np.where(rel > 0, half, 0) + np.where(n < max_exact, n, large)).astype(np.int32)


def hgrn2_scan(q, k, v, logf):
    Bsz, S, H, dk = q.shape
    dv = v.shape[-1]
    C = A_CHUNK
    n = S // C

    def chunks(t):
        return t.reshape(Bsz, n, C, H, t.shape[-1]).transpose(1, 0, 3, 2, 4)

    lower = jnp.tril(jnp.ones((C, C), dtype=bool))[:, :, None]

    def step(state, inp):
        qc, kc, vc, lc = inp
        b = jnp.cumsum(lc, axis=2)
        o_inter = jnp.einsum('bhtd,bhdv->bhtv', qc * jnp.exp(b), state)
        diff = jnp.where(lower, b[:, :, :, None, :] - b[:, :, None, :, :], -jnp.inf)
        scores = jnp.einsum('bhtd,bhsd,bhtsd->bhts', qc, kc, jnp.exp(diff))
        o_intra = jnp.einsum('bhts,bhsv->bhtv', scores, vc)
        b_last = b[:, :, -1:, :]
        new_state = (jnp.exp(b_last[:, :, 0, :, None]) * state
                     + jnp.einsum('bhsd,bhsv->bhdv', kc * jnp.exp(b_last - b), vc))
        return new_state, o_inter + o_intra

    state0 = jnp.zeros((Bsz, H, dk, dv), jnp.float32)
    _, o = lax.scan(step, state0, (chunks(q), chunks(k), chunks(v), chunks(logf)))
    return o.transpose(1, 0, 3, 2, 4).reshape(Bsz, S, H, dv)


def hgrn2_mixer(xa, g_norm, lb):
    q, i, zf, zb, g = jnp.split(xa, 5, axis=-1)
    Bsz, S, _ = q.shape

    def heads(t):
        return t.reshape(Bsz, S, A_HEADS, A_HEAD_DIM)

    qh = heads(jax.nn.silu(q.astype(jnp.float32)))
    vh = heads(i.astype(jnp.float32))

    def gates(z, lb_d):
        z = z.astype(jnp.float32)
        logf = jnp.log(lb_d + (1.0 - lb_d) * jax.nn.sigmoid(z))
        k = (1.0 - lb_d) * jax.nn.sigmoid(-z)
        return heads(k), heads(logf)

    kf, lf = gates(zf, lb[0])
    kb, lbw = gates(zb, lb[1])
    o_f = hgrn2_scan(qh, kf, vh, lf)

    def flip(t):
        return jnp.flip(t, axis=1)

    o_b = flip(hgrn2_scan(flip(qh), flip(kb), flip(vh), flip(lbw)))
    o = o_f + o_b
    o = o * lax.rsqrt(jnp.mean(o * o, -1, keepdims=True) + RMS_EPS)
    o = o.reshape(Bsz, S, A_WIDTH) * g_norm.astype(jnp.float32) * jax.nn.silu(g.astype(jnp.float32))
    return o.astype(xa.dtype)


def dilated_group(q, k, v, bias_table, window, r):
    Bsz, S, H, dh = q.shape
    QB = B_QBLOCK
    half = window // (2 * r)
    L = S // r
    nb = -(-L // QB)
    Lp = nb * QB

    def sub(t):
        return t.reshape(Bsz, L, r, H, dh).transpose(0, 2, 3, 1, 4)

    qb = jnp.pad(sub(q), ((0, 0), (0, 0), (0, 0), (0, Lp - L), (0, 0))).reshape(Bsz, r, H, nb, QB, dh)

    def band(t):
        tp = jnp.pad(sub(t), ((0, 0), (0, 0), (0, 0), (QB, Lp - L + QB), (0, 0)))
        tp = tp.reshape(Bsz, r, H, nb + 2, QB, dh)
        return jnp.concatenate([tp[:, :, :, :-2], tp[:, :, :, 1:-1], tp[:, :, :, 2:]], axis=4)

    kb, vb = band(k), band(v)
    a = np.arange(QB)[:, None]
    c = np.arange(3 * QB)[None, :]
    rel = c - QB - a
    key_idx = np.arange(nb)[:, None, None] * QB - QB + c[None]
    valid = (np.abs(rel) <= half)[None] & (key_idx >= 0) & (key_idx < L)
    bias = jnp.transpose(bias_table[t5_buckets(rel * r)], (2, 0, 1)).astype(jnp.float32)
    s = jnp.einsum('brhnqd,brhnkd->brhnqk', qb, kb) * (dh ** -0.5) + bias[:, None]
    s = jnp.where(valid, s, NEG_INF)
    m = jnp.max(s, -1, keepdims=True)
    p = jnp.exp(s - m)
    den = jnp.sum(p, -1, keepdims=True)
    o = jnp.einsum('brhnqk,brhnkd->brhnqd', p, vb) / den
    lse = (m + jnp.log(den))[..., 0]
    o = o.reshape(Bsz, r, H, Lp, dh)[:, :, :, :L].transpose(0, 3, 1, 2, 4).reshape(Bsz, S, H, dh)
    lse = lse.reshape(Bsz, r, H, Lp)[:, :, :, :L].transpose(0, 3, 1, 2).reshape(Bsz, S, H)
    return o, lse


def dilated_mixer(xb, rel_bias):
    Bsz, S, _ = xb.shape
    q, k, v = [t.astype(jnp.float32).reshape(Bsz, S, B_HEADS, B_HEAD_DIM) for t in jnp.split(xb, 3, axis=-1)]
    outs, lses = [], []
    for gi, (window, r) in enumerate(B_GROUPS):
        hs = slice(gi * B_HEADS_PER_GROUP, (gi + 1) * B_HEADS_PER_GROUP)
        o, lse = dilated_group(q[:, :, hs], k[:, :, hs], v[:, :, hs], rel_bias[:, hs], window, r)
        outs.append(o)
        lses.append(lse)
    o = jnp.stack(outs, axis=2)
    weight = jax.nn.softmax(jnp.stack(lses, axis=2), axis=2)
    return (o * weight[..., None]).reshape(Bsz, S, B_WIDTH).astype(xb.dtype)


def ab_mixer(x, w_in, g_norm, w_out, rel_bias, lb):
    h = x @ w_in
    o_a = hgrn2_mixer(h[..., :5 * A_WIDTH], g_norm, lb)
    o_b = dilated_mixer(h[..., 5 * A_WIDTH:], rel_bias)
    return jnp.concatenate([o_a, o_b], axis=-1) @ w_out


def conv_mixer(x, w_in, b_in, dw_w, dw_b, n_g, n_b, w_out, b_out):
    h = x @ w_in + b_in
    a, gate = jnp.split(h, 2, axis=-1)
    u = a * jax.nn.sigmoid(gate)
    u = lax.conv_general_dilated(u, dw_w[:, None, :].astype(u.dtype), window_strides=(1,), padding='SAME',
                                 dimension_numbers=('NWC', 'WIO', 'NWC'),
                                 feature_group_count=C_WIDTH) + dw_b
    u = jax.nn.silu(layer_norm(u, n_g, n_b))
    return u @ w_out + b_out


def sq_relu_mlp(x, w1, w2):
    return jnp.square(jax.nn.relu(x @ w1)) @ w2


def trunk(x, rel_bias, hgrn_lb, w_in_ab, hgrn_norm, w_out_ab, w_in_c, b_in_c, dw_c, dw_b_c, cnorm_g,
          cnorm_b, w_out_c, b_out_c, ln_mix_g, ln_mix_b, mlp_w1, mlp_w2, ln_ffn_g, ln_ffn_b):
    lb_all = jnp.cumsum(jax.nn.softmax(hgrn_lb.astype(jnp.float32), axis=1), axis=1)
    for l in range(DEPTH):
        i = l // 2
        if l % 2 == 0:
            mix = ab_mixer(x, w_in_ab[i], hgrn_norm[i], w_out_ab[i], rel_bias, lb_all[:, l])
        else:
            mix = conv_mixer(x, w_in_c[i], b_in_c[i], dw_c[i], dw_b_c[i], cnorm_g[i], cnorm_b[i],
                             w_out_c[i], b_out_c[i])
        x = layer_norm(ALPHA * x + mix, ln_mix_g[l], ln_mix_b[l])
        x = layer_norm(ALPHA * x + sq_relu_mlp(x, mlp_w1[l], mlp_w2[l]), ln_ffn_g[l], ln_ffn_b[l])
    return x


def setup_inputs(seed: int = 0) -> dict:
    key = jax.random.key(seed)
    ks = jax.random.split(key, 21)

    def nrm(k, shape, scale):
        return jax.random.normal(k, shape, jnp.float32) * scale

    return {
        'x_prompt': nrm(ks[0], (BATCH, SEQ, D_MODEL), 1.0),
        'x_sample': nrm(ks[1], (DEC_BATCH, DEC_SEQ, D_MODEL), 1.0),
        'rel_bias': nrm(ks[2], (REL_BUCKETS, B_HEADS), 0.5),
        'hgrn_lb': nrm(ks[3], (2, DEPTH + 1, A_WIDTH), 0.5),
        'w_in_ab': nrm(ks[4], (N_AB, D_MODEL, AB_IN), D_MODEL ** -0.5),
        'hgrn_norm': 1.0 + nrm(ks[5], (N_AB, A_WIDTH), 0.05),
        'w_out_ab': nrm(ks[6], (N_AB, AB_OUT, D_MODEL), BETA * AB_OUT ** -0.5),
        'w_in_c': nrm(ks[7], (N_C, D_MODEL, 2 * C_WIDTH), D_MODEL ** -0.5),
        'b_in_c': nrm(ks[8], (N_C, 2 * C_WIDTH), 0.02),
        'dw_c': nrm(ks[9], (N_C, CONV_WIDTH, C_WIDTH), CONV_WIDTH ** -0.5),
        'dw_b_c': nrm(ks[10], (N_C, C_WIDTH), 0.02),
        'cnorm_g': 1.0 + nrm(ks[11], (N_C, C_WIDTH), 0.05),
        'cnorm_b': nrm(ks[12], (N_C, C_WIDTH), 0.02),
        'w_out_c': nrm(ks[13], (N_C, C_WIDTH, D_MODEL), BETA * C_WIDTH ** -0.5),
        'b_out_c': nrm(ks[14], (N_C, D_MODEL), 0.02),
        'ln_mix_g': 1.0 + nrm(ks[15], (DEPTH, D_MODEL), 0.05),
        'ln_mix_b': nrm(ks[16], (DEPTH, D_MODEL), 0.02),
        'mlp_w1': nrm(ks[17], (DEPTH, D_MODEL, D_FF), D_MODEL ** -0.5),
        'mlp_w2': nrm(ks[18], (DEPTH, D_FF, D_MODEL), BETA * D_FF ** -0.5),
        'ln_ffn_g': 1.0 + nrm(ks[19], (DEPTH, D_MODEL), 0.05),
        'ln_ffn_b': nrm(ks[20], (DEPTH, D_MODEL), 0.02),
    }


def reference(x_prompt, x_sample, rel_bias, hgrn_lb, w_in_ab, hgrn_norm, w_out_ab, w_in_c, b_in_c, dw_c,
              dw_b_c, cnorm_g, cnorm_b, w_out_c, b_out_c, ln_mix_g, ln_mix_b, mlp_w1, mlp_w2, ln_ffn_g,
              ln_ffn_b):
    y_prompt = trunk(x_prompt, rel_bias, hgrn_lb, w_in_ab, hgrn_norm, w_out_ab, w_in_c, b_in_c, dw_c, dw_b_c,
                     cnorm_g, cnorm_b, w_out_c, b_out_c, ln_mix_g, ln_mix_b, mlp_w1, mlp_w2, ln_ffn_g, ln_ffn_b)
    y_sample = trunk(x_sample, rel_bias, hgrn_lb, w_in_ab, hgrn_norm, w_out_ab, w_in_c, b_in_c, dw_c, dw_b_c,
                     cnorm_g, cnorm_b, w_out_c, b_out_c, ln_mix_g, ln_mix_b, mlp_w1, mlp_w2, ln_ffn_g, ln_ffn_b)
    return (y_prompt, y_sample)
```

```python
import functools

import numpy as np
import jax
import jax.numpy as jnp
from jax import lax
from jax.experimental import pallas as pl
from jax.experimental.pallas import tpu as pltpu

F32 = jnp.float32
BF16 = jnp.bfloat16

D_MODEL = 1024
DEPTH = 2
A_HEADS = 4
A_HEAD_DIM = 128
A_WIDTH = A_HEADS * A_HEAD_DIM
B_GROUPS = ((128, 1), (512, 4), (2048, 16))
B_HEADS_PER_GROUP = 4
B_HEAD_DIM = 64
B_GROUP_WIDTH = B_HEADS_PER_GROUP * B_HEAD_DIM
B_WIDTH = B_GROUP_WIDTH * len(B_GROUPS)
CONV_WIDTH = 31
D_FF = 4 * D_MODEL
REL_BUCKETS = 32
REL_MAX_DIST = 1024
ALPHA = (2 * DEPTH) ** 0.25
LN_EPS = 1e-5
RMS_EPS = 1e-6
NEG_INF = -1e30

SUBLANES = 8
HGRN_CHUNK = 128
ATT_QBLOCK = 128
CONV_HALO = 16
CONV_ROWS = 16
CONV_LANES = 128
VMEM_LIMIT = 52 * 1024 * 1024


def _cparams(*sem):
    return pltpu.CompilerParams(dimension_semantics=sem, vmem_limit_bytes=VMEM_LIMIT)


def _dot(a, b):
    return jnp.dot(a, b, preferred_element_type=F32)


def _dot_nt(a, b):
    return lax.dot_general(a, b, (((1,), (1,)), ((), ())), preferred_element_type=F32)


def _dot_tn(a, b):
    return lax.dot_general(a, b, (((0,), (0,)), ((), ())), preferred_element_type=F32)


def _layer_norm(y, g, b):
    mu = jnp.mean(y, -1, keepdims=True)
    yc = y - mu
    var = jnp.mean(jnp.square(yc), -1, keepdims=True)
    return yc * lax.rsqrt(var + LN_EPS) * g + b


def _mm_body(x_ref, w_ref, o_ref):
    o_ref[...] = _dot(x_ref[...].astype(BF16), w_ref[...]).astype(o_ref.dtype)


def _matmul(x, w, out_dtype, tm, tn):
    m, k = x.shape
    n = w.shape[1]
    return pl.pallas_call(
        _mm_body,
        grid=(m // tm, n // tn),
        in_specs=[pl.BlockSpec((tm, k), lambda i, j: (i, 0)),
                  pl.BlockSpec((k, tn), lambda i, j: (0, j))],
        out_specs=pl.BlockSpec((tm, tn), lambda i, j: (i, j)),
        out_shape=jax.ShapeDtypeStruct((m, n), out_dtype),
        compiler_params=_cparams("parallel", "arbitrary"),
        name="in_proj",
    )(x, w)


def _hgrn_body(q_ref, v_ref, zf_ref, zb_ref, g_ref, lbp_ref, gn_ref, o_ref,
               qs, kf, kb, bf, cb, of, ob, sf, sb, *, seq, chunk, layer):
    C = chunk
    n_chunks = seq // C
    dk = A_HEAD_DIM
    row = lax.broadcasted_iota(jnp.int32, (C, dk), 0)

    def lower_bound(direction):
        p = [lbp_ref[direction, l:l + 1, :] for l in range(DEPTH + 1)]
        m = functools.reduce(jnp.maximum, p)
        e = [jnp.exp(v - m) for v in p]
        return sum(e[:layer + 1]) / sum(e)

    lb_f = lower_bound(0)
    lb_b = lower_bound(1)

    def gates(z, lb_d):
        logf = jnp.log(lb_d + (1.0 - lb_d) * jax.nn.sigmoid(z))
        k = (1.0 - lb_d) * jax.nn.sigmoid(-z)
        return k, logf

    def prep(n, carry):
        rows = pl.ds(pl.multiple_of(n * C, C), C)
        q = q_ref[0, rows, :]
        qs[rows, :] = q * jax.nn.sigmoid(q)
        k, lf = gates(zf_ref[0, rows, :], lb_f)
        kf[rows, :] = k
        s = 1
        while s < C:
            lf = lf + jnp.where(row >= s, pltpu.roll(lf, s, 0), 0.0)
            s *= 2
        bf[rows, :] = lf
        k, lg = gates(zb_ref[0, rows, :], lb_b)
        kb[rows, :] = k
        s = 1
        while s < C:
            lg = lg + jnp.where(row < C - s, pltpu.roll(lg, C - s, 0), 0.0)
            s *= 2
        cb[rows, :] = lg
        return carry

    lax.fori_loop(0, n_chunks, prep, 0)

    sf[...] = jnp.zeros_like(sf)
    sb[...] = jnp.zeros_like(sb)
    sub = lax.broadcasted_iota(jnp.int32, (C // SUBLANES, SUBLANES, dk), 1)
    trow = lax.broadcasted_iota(jnp.int32, (C, C), 0)
    tcol = lax.broadcasted_iota(jnp.int32, (C, C), 1)

    def intra(q, kfv, kbv, bfv, cbv, v):
        grp = (C // SUBLANES, SUBLANES, dk)
        acc = jnp.sum(q * (kfv + kbv), axis=-1, keepdims=True) * v
        q3, v3 = q.reshape(grp), v.reshape(grp)
        kf3, kb3, bf3, cb3 = kfv.reshape(grp), kbv.reshape(grp), bfv.reshape(grp), cbv.reshape(grp)
        acc3 = acc.reshape(grp)
        for d in range(1, SUBLANES):
            src_fwd = sub <= SUBLANES - 1 - d
            kr = pltpu.roll(jnp.where(src_fwd, kf3, kb3), d, 1)
            br = pltpu.roll(jnp.where(src_fwd, bf3, cb3), d, 1)
            vr = pltpu.roll(v3, d, 1)
            ex = jnp.exp(jnp.where(sub >= d, bf3, cb3) - br)
            a = jnp.sum(q3 * kr * ex, axis=-1, keepdims=True)
            acc3 = acc3 + a * vr
        acc = acc3.reshape(C, dk)
        scores = jnp.zeros((C, C), F32)
        w = SUBLANES
        while w < C:
            lg = w.bit_length() - 1
            nblk = C // (2 * w)
            rbf = jnp.concatenate(
                [jnp.broadcast_to(bfv[b * 2 * w + w:b * 2 * w + w + 1, :], (2 * w, dk)) for b in range(nblk)], 0)
            rcb = jnp.concatenate(
                [jnp.broadcast_to(cbv[b * 2 * w + w:b * 2 * w + w + 1, :], (2 * w, dk)) for b in range(nblk)], 0)
            second = ((row >> lg) & 1) == 1
            qt = q * jnp.exp(jnp.where(second, bfv - rbf, cbv - rcb))
            kt = jnp.where(second, kbv, kfv) * jnp.exp(jnp.where(second, rcb - cbv, rbf - bfv))
            p = _dot_nt(qt.astype(BF16), kt.astype(BF16))
            scores = scores + jnp.where(((trow >> lg) ^ (tcol >> lg)) == 1, p, 0.0)
            w *= 2
        return acc + _dot(scores.astype(BF16), v.astype(BF16))

    def step(n, carry):
        rows = pl.ds(pl.multiple_of(n * C, C), C)
        q, kfv, kbv, bfv, cbv, v = qs[rows, :], kf[rows, :], kb[rows, :], bf[rows, :], cb[rows, :], v_ref[0, rows, :]
        o = intra(q, kfv, kbv, bfv, cbv, v)
        st = sf[...]
        o = o + _dot_nt((q * jnp.exp(bfv)).astype(BF16), st.astype(BF16))
        of[rows, :] = o
        b_last = bfv[C - 1:C, :]
        kt = (kfv * jnp.exp(b_last - bfv)).astype(BF16)
        sf[...] = jnp.exp(b_last) * st + _dot_tn(v.astype(BF16), kt)
        rows = pl.ds(pl.multiple_of((n_chunks - 1 - n) * C, C), C)
        q, kbv, cbv, v = qs[rows, :], kb[rows, :], cb[rows, :], v_ref[0, rows, :]
        st = sb[...]
        ob[rows, :] = _dot_nt((q * jnp.exp(cbv)).astype(BF16), st.astype(BF16))
        c_first = cbv[0:1, :]
        kt = (kbv * jnp.exp(c_first - cbv)).astype(BF16)
        sb[...] = jnp.exp(c_first) * st + _dot_tn(v.astype(BF16), kt)
        return carry

    lax.fori_loop(0, n_chunks, step, 0)

    gn = gn_ref[...]

    def finish(n, carry):
        rows = pl.ds(pl.multiple_of(n * C, C), C)
        o = of[rows, :] + ob[rows, :]
        o = o * lax.rsqrt(jnp.mean(o * o, -1, keepdims=True) + RMS_EPS)
        g = g_ref[0, rows, :]
        o_ref[0, rows, :] = (o * gn * (g * jax.nn.sigmoid(g))).astype(o_ref.dtype)
        return carry

    lax.fori_loop(0, n_chunks, finish, 0)


def _hgrn_mixer(ha, hgrn_lb, g_norm, layer):
    bsz, seq, _ = ha.shape
    dk = A_HEAD_DIM
    chunk = min(HGRN_CHUNK, seq)

    def col(part):
        return pl.BlockSpec((1, seq, dk), lambda b, h, part=part: (b, 0, part * A_HEADS + h))

    body = functools.partial(_hgrn_body, seq=seq, chunk=chunk, layer=layer)
    return pl.pallas_call(
        body,
        grid=(bsz, A_HEADS),
        in_specs=[col(0), col(1), col(2), col(3), col(4),
                  pl.BlockSpec((2, DEPTH + 1, dk), lambda b, h: (0, 0, h)),
                  pl.BlockSpec((1, dk), lambda b, h: (0, h))],
        out_specs=pl.BlockSpec((1, seq, dk), lambda b, h: (b, 0, h)),
        out_shape=jax.ShapeDtypeStruct((bsz, seq, A_WIDTH), BF16),
        scratch_shapes=[pltpu.VMEM((seq, dk), F32)] * 7 + [pltpu.VMEM((dk, dk), F32)] * 2,
        compiler_params=_cparams("parallel", "arbitrary"),
        name="hgrn2",
    )(ha, ha, ha, ha, ha, hgrn_lb, g_norm.reshape(1, A_WIDTH))


def _t5_buckets(rel):
    half = REL_BUCKETS // 2
    max_exact = half // 2
    n = np.abs(rel)
    large = max_exact + (np.log(np.maximum(n, 1) / max_exact) / np.log(REL_MAX_DIST / max_exact)
                         * (half - max_exact)).astype(np.int32)
    large = np.minimum(large, half - 1)
    return (np.where(rel > 0, half, 0) + np.where(n < max_exact, n, large)).astype(np.int32)


def _att_geometry(sub_len, half):
    lq = min(ATT_QBLOCK, sub_len)
    win = min(lq + 2 * half, sub_len)
    return lq, win, sub_len // lq


def _att_bias(rel_bias_g, sub_len, dilation, half):
    lq, win, _ = _att_geometry(sub_len, half)
    tables = []
    for delta in (0, -half, lq - win):
        rel = np.arange(win)[None, :] + delta - np.arange(lq)[:, None]
        valid = np.abs(rel) <= half
        bias = jnp.transpose(rel_bias_g[_t5_buckets(rel * dilation)], (2, 0, 1)).astype(F32)
        tables.append(jnp.where(valid[None], bias, NEG_INF))
    return jnp.stack(tables)


def _att_body(q_ref, k_ref, v_ref, bias_ref, o_ref, lse_ref, *, sub_len, lq, win, half):
    iq = pl.program_id(2)
    ws = jnp.clip(iq * lq - half, 0, sub_len - win)
    ws = pl.multiple_of(ws, half)
    kwin = k_ref[0, pl.ds(ws, win), :]
    vwin = v_ref[0, pl.ds(ws, win), :]
    q = q_ref[0] * (B_HEAD_DIM ** -0.5)
    lane = lax.broadcasted_iota(jnp.int32, (lq, B_GROUP_WIDTH), 1)
    out = jnp.zeros((lq, B_GROUP_WIDTH), F32)
    lse = jnp.zeros((lq, B_GROUP_WIDTH), F32)
    for h in range(B_HEADS_PER_GROUP):
        mine = (lane >= h * B_HEAD_DIM) & (lane < (h + 1) * B_HEAD_DIM)
        s = _dot_nt(jnp.where(mine, q, jnp.zeros_like(q)), kwin) + bias_ref[0, h]
        m = jnp.max(s, -1, keepdims=True)
        p = jnp.exp(s - m)
        den = jnp.sum(p, -1, keepdims=True)
        o = _dot(p.astype(BF16), vwin) / den
        out = jnp.where(mine, o, out)
        lse = jnp.where(mine, m + jnp.log(den), lse)
    o_ref[0] = out.astype(o_ref.dtype)
    lse_ref[0] = lse


def _dilated_group(hb, rel_bias_g, gi, window, dilation):
    bsz, seq, _ = hb.shape
    r = dilation
    sub_len = seq // r
    half = window // (2 * r)
    lq, win, nq = _att_geometry(sub_len, half)
    gw = B_GROUP_WIDTH
    blocks_per_res = 3 * B_WIDTH // gw
    hv = hb.reshape(bsz, sub_len, r * 3 * B_WIDTH)
    bias = _att_bias(rel_bias_g, sub_len, r, half)

    def colblock(part):
        return lambda b, c, i: (b, 0, c * blocks_per_res + part * len(B_GROUPS) + gi)

    def case(b, c, i):
        return (jnp.where(i == 0, 0, jnp.where(i == nq - 1, 2, 1)), 0, 0, 0)

    body = functools.partial(_att_body, sub_len=sub_len, lq=lq, win=win, half=half)
    o, lse = pl.pallas_call(
        body,
        grid=(bsz, r, nq),
        in_specs=[pl.BlockSpec((1, lq, gw), lambda b, c, i: (b, i, c * blocks_per_res + gi)),
                  pl.BlockSpec((1, sub_len, gw), colblock(1)),
                  pl.BlockSpec((1, sub_len, gw), colblock(2)),
                  pl.BlockSpec((1, B_HEADS_PER_GROUP, lq, win), case)],
        out_specs=[pl.BlockSpec((1, lq, gw), lambda b, c, i: (b, i, c)),
                   pl.BlockSpec((1, lq, gw), lambda b, c, i: (b, i, c))],
        out_shape=[jax.ShapeDtypeStruct((bsz, sub_len, r * gw), BF16),
                   jax.ShapeDtypeStruct((bsz, sub_len, r * gw), F32)],
        compiler_params=_cparams("parallel", "arbitrary", "arbitrary"),
        name=f"dilated_attn_r{r}",
    )(hv, hv, hv, bias)
    return o.reshape(bsz * seq, gw), lse.reshape(bsz * seq, gw)


def _ab_out_body(oa_ref, o1_ref, o2_ref, o3_ref, l1_ref, l2_ref, l3_ref, x_ref, w_ref, g_ref, b_ref, y_ref):
    l1, l2, l3 = l1_ref[...], l2_ref[...], l3_ref[...]
    m = jnp.maximum(jnp.maximum(l1, l2), l3)
    e1, e2, e3 = jnp.exp(l1 - m), jnp.exp(l2 - m), jnp.exp(l3 - m)
    tot = e1 + e2 + e3
    mix = _dot(oa_ref[...], w_ref[0:A_WIDTH, :])
    for gi, (o_ref, e) in enumerate(((o1_ref, e1), (o2_ref, e2), (o3_ref, e3))):
        og = (o_ref[...].astype(F32) * (e / tot)).astype(BF16)
        lo = A_WIDTH + gi * B_GROUP_WIDTH
        mix = mix + _dot(og, w_ref[lo:lo + B_GROUP_WIDTH, :])
    y_ref[...] = _layer_norm(ALPHA * x_ref[...] + mix, g_ref[...], b_ref[...])


def _ab_out(oa, outs, lses, x, w_out, ln_g, ln_b, tm):
    t = x.shape[0]
    gw = B_GROUP_WIDTH

    def rowblock(width):
        return pl.BlockSpec((tm, width), lambda i: (i, 0))

    def whole(shape):
        return pl.BlockSpec(shape, lambda i: (0, 0))

    return pl.pallas_call(
        _ab_out_body,
        grid=(t // tm,),
        in_specs=[rowblock(A_WIDTH)] + [rowblock(gw)] * 6 + [rowblock(D_MODEL), whole(w_out.shape),
                                                           whole((1, D_MODEL)), whole((1, D_MODEL))],
        out_specs=rowblock(D_MODEL),
        out_shape=jax.ShapeDtypeStruct((t, D_MODEL), F32),
        compiler_params=_cparams("parallel"),
        name="ab_out_ln",
    )(oa, *outs, *lses, x, w_out, ln_g.reshape(1, D_MODEL), ln_b.reshape(1, D_MODEL))


def _mlp_body(x_ref, w1_ref, w2_ref, g_ref, b_ref, y_ref, xb, acc):
    j = pl.program_id(1)

    @pl.when(j == 0)
    def _():
        xb[...] = x_ref[...].astype(BF16)
        acc[...] = jnp.zeros_like(acc)

    h = jnp.square(jnp.maximum(_dot(xb[...], w1_ref[...]), 0.0))
    acc[...] += _dot(h.astype(BF16), w2_ref[...])

    @pl.when(j == pl.num_programs(1) - 1)
    def _():
        y_ref[...] = _layer_norm(ALPHA * x_ref[...] + acc[...], g_ref[...], b_ref[...])


def _mlp(x, w1, w2, ln_g, ln_b, tm, tf):
    t = x.shape[0]
    return pl.pallas_call(
        _mlp_body,
        grid=(t // tm, D_FF // tf),
        in_specs=[pl.BlockSpec((tm, D_MODEL), lambda i, j: (i, 0)),
                  pl.BlockSpec((D_MODEL, tf), lambda i, j: (0, j)),
                  pl.BlockSpec((tf, D_MODEL), lambda i, j: (j, 0)),
                  pl.BlockSpec((1, D_MODEL), lambda i, j: (0, 0)),
                  pl.BlockSpec((1, D_MODEL), lambda i, j: (0, 0))],
        out_specs=pl.BlockSpec((tm, D_MODEL), lambda i, j: (i, 0)),
        out_shape=jax.ShapeDtypeStruct((t, D_MODEL), F32),
        scratch_shapes=[pltpu.VMEM((tm, D_MODEL), BF16), pltpu.VMEM((tm, D_MODEL), F32)],
        compiler_params=_cparams("parallel", "arbitrary"),
        name="mlp_ln",
    )(x, w1, w2, ln_g.reshape(1, D_MODEL), ln_b.reshape(1, D_MODEL))


def _glu_body(x_ref, wa_ref, wg_ref, ba_ref, bg_ref, u_ref):
    xb = x_ref[...].astype(BF16)
    a = _dot(xb, wa_ref[...]) + ba_ref[...]
    gate = _dot(xb, wg_ref[...]) + bg_ref[...]
    u_ref[...] = a * jax.nn.sigmoid(gate)


def _glu_proj(x, w_in, b_in, tm, tn):
    t = x.shape[0]
    nb = D_MODEL // tn
    b2 = b_in.reshape(1, 2 * D_MODEL)
    return pl.pallas_call(
        _glu_body,
        grid=(t // tm, nb),
        in_specs=[pl.BlockSpec((tm, D_MODEL), lambda i, j: (i, 0)),
                  pl.BlockSpec((D_MODEL, tn), lambda i, j: (0, j)),
                  pl.BlockSpec((D_MODEL, tn), lambda i, j: (0, j + nb)),
                  pl.BlockSpec((1, tn), lambda i, j: (0, j)),
                  pl.BlockSpec((1, tn), lambda i, j: (0, j + nb))],
        out_specs=pl.BlockSpec((tm, tn), lambda i, j: (i, j)),
        out_shape=jax.ShapeDtypeStruct((t, D_MODEL), F32),
        compiler_params=_cparams("parallel", "arbitrary"),
        name="conv_glu",
    )(x, w_in, w_in, b2, b2)


def _conv_out_body(u_ref, up_ref, un_ref, x_ref, dw_ref, dwb_ref, ng_ref, nb_ref, w_ref, bo_ref, g_ref, b_ref,
                   y_ref, buf, act, *, ts):
    i = pl.program_id(1)
    halo = CONV_HALO
    pad = CONV_WIDTH // 2
    buf[0:halo, :] = jnp.where(i > 0, up_ref[0], 0.0)
    buf[halo:halo + ts, :] = u_ref[0]
    buf[halo + ts:halo + ts + halo, :] = jnp.where(i < pl.num_programs(1) - 1, un_ref[0], 0.0)
    dwb, ng, nb = dwb_ref[...], ng_ref[...], nb_ref[...]

    def tile(rt, carry):
        base = pl.multiple_of(rt * CONV_ROWS, CONV_ROWS)
        cols = []
        for c0 in range(0, D_MODEL, CONV_LANES):
            win = buf[pl.ds(base, CONV_ROWS + 2 * halo), c0:c0 + CONV_LANES]
            acc = jnp.zeros((CONV_ROWS, CONV_LANES), F32)
            for j in range(CONV_WIDTH):
                lo = halo - pad + j
                acc = acc + dw_ref[j:j + 1, c0:c0 + CONV_LANES] * win[lo:lo + CONV_ROWS, :]
            cols.append(acc)
        c = _layer_norm(jnp.concatenate(cols, axis=1) + dwb, ng, nb)
        act[pl.ds(base, CONV_ROWS), :] = (c * jax.nn.sigmoid(c)).astype(BF16)
        return carry

    lax.fori_loop(0, ts // CONV_ROWS, tile, 0)
    mix = _dot(act[...], w_ref[...]) + bo_ref[...]
    y_ref[0] = _layer_norm(ALPHA * x_ref[0] + mix, g_ref[...], b_ref[...])


def _conv_out(u, x, dw, dw_b, n_g, n_b, w_out, b_out, ln_g, ln_b, ts):
    bsz, seq, d = x.shape
    hb = ts // CONV_HALO
    last_halo = seq // CONV_HALO - 1

    def vec():
        return pl.BlockSpec((1, d), lambda b, i: (0, 0))

    body = functools.partial(_conv_out_body, ts=ts)
    return pl.pallas_call(
        body,
        grid=(bsz, seq // ts),
        in_specs=[pl.BlockSpec((1, ts, d), lambda b, i: (b, i, 0)),
                  pl.BlockSpec((1, CONV_HALO, d), lambda b, i: (b, jnp.maximum(i * hb - 1, 0), 0)),
                  pl.BlockSpec((1, CONV_HALO, d), lambda b, i: (b, jnp.minimum((i + 1) * hb, last_halo), 0)),
                  pl.BlockSpec((1, ts, d), lambda b, i: (b, i, 0)),
                  pl.BlockSpec((CONV_WIDTH, d), lambda b, i: (0, 0)),
                  vec(), vec(), vec(),
                  pl.BlockSpec((d, d), lambda b, i: (0, 0)),
                  vec(), vec(), vec()],
        out_specs=pl.BlockSpec((1, ts, d), lambda b, i: (b, i, 0)),
        out_shape=jax.ShapeDtypeStruct((bsz, seq, d), F32),
        scratch_shapes=[pltpu.VMEM((ts + 2 * CONV_HALO, d), F32), pltpu.VMEM((ts, d), BF16)],
        compiler_params=_cparams("parallel", "arbitrary"),
        name="conv_out_ln",
    )(u, u, u, x, dw, dw_b.reshape(1, d), n_g.reshape(1, d), n_b.reshape(1, d), w_out,
      b_out.reshape(1, d), ln_g.reshape(1, d), ln_b.reshape(1, d))


def _tiles(tokens):
    tm = 1024 if tokens % 1024 == 0 else 256
    return tm


def _trunk(x, rel_bias, hgrn_lb, w_in_ab, hgrn_norm, w_out_ab, w_in_c, b_in_c, dw_c, dw_b_c, cnorm_g, cnorm_b,
           w_out_c, b_out_c, ln_mix_g, ln_mix_b, mlp_w1, mlp_w2, ln_ffn_g, ln_ffn_b):
    bsz, seq, d = x.shape
    t = bsz * seq
    tm = _tiles(t)
    x2 = x.reshape(t, d)
    for l in range(DEPTH):
        i = l // 2
        if l % 2 == 0:
            wa = w_in_ab[i][:, :5 * A_WIDTH].astype(BF16)
            wb = w_in_ab[i][:, 5 * A_WIDTH:].astype(BF16)
            ha = _matmul(x2, wa, F32, tm, 5 * A_WIDTH // 2)
            hb = _matmul(x2, wb, BF16, tm, 3 * B_WIDTH // 2)
            oa = _hgrn_mixer(ha.reshape(bsz, seq, 5 * A_WIDTH), hgrn_lb, hgrn_norm[i], l)
            hb3 = hb.reshape(bsz, seq, 3 * B_WIDTH)
            outs, lses = [], []
            for gi, (window, r) in enumerate(B_GROUPS):
                hs = slice(gi * B_HEADS_PER_GROUP, (gi + 1) * B_HEADS_PER_GROUP)
                o, lse = _dilated_group(hb3, rel_bias[:, hs], gi, window, r)
                outs.append(o)
                lses.append(lse)
            x2 = _ab_out(oa.reshape(t, A_WIDTH), outs, lses, x2, w_out_ab[i].astype(BF16),
                         ln_mix_g[l], ln_mix_b[l], tm)
        else:
            u = _glu_proj(x2, w_in_c[i].astype(BF16), b_in_c[i], tm, D_MODEL // 2)
            x2 = _conv_out(u.reshape(bsz, seq, d), x2.reshape(bsz, seq, d), dw_c[i], dw_b_c[i], cnorm_g[i],
                           cnorm_b[i], w_out_c[i].astype(BF16), b_out_c[i], ln_mix_g[l], ln_mix_b[l],
                           min(512, seq)).reshape(t, d)
        x2 = _mlp(x2, mlp_w1[l].astype(BF16), mlp_w2[l].astype(BF16), ln_ffn_g[l], ln_ffn_b[l], tm, 1024)
    return x2.reshape(bsz, seq, d)


def kernel(x_prompt, x_sample, rel_bias, hgrn_lb, w_in_ab, hgrn_norm, w_out_ab, w_in_c, b_in_c, dw_c, dw_b_c,
           cnorm_g, cnorm_b, w_out_c, b_out_c, ln_mix_g, ln_mix_b, mlp_w1, mlp_w2, ln_ffn_g, ln_ffn_b):
    params = (rel_bias, hgrn_lb, w_in_ab, hgrn_norm, w_out_ab, w_in_c, b_in_c, dw_c, dw_b_c, cnorm_g, cnorm_b,
              w_out_c, b_out_c, ln_mix_g, ln_mix_b, mlp_w1, mlp_w2, ln_ffn_g, ln_ffn_b)
    return (_trunk(x_prompt, *params), _trunk(x_sample, *params))
```

```python
import functools

import numpy as np
import jax
import jax.numpy as jnp
from jax import lax
from jax.experimental import pallas as pl
from jax.experimental.pallas import tpu as pltpu

F32 = jnp.float32
BF16 = jnp.bfloat16

D_MODEL = 1024
DEPTH = 2
A_HEADS = 4
A_HEAD_DIM = 128
A_WIDTH = A_HEADS * A_HEAD_DIM
B_GROUPS = ((128, 1), (512, 4), (2048, 16))
B_HEADS_PER_GROUP = 4
B_HEAD_DIM = 64
B_GROUP_WIDTH = B_HEADS_PER_GROUP * B_HEAD_DIM
B_WIDTH = B_GROUP_WIDTH * len(B_GROUPS)
CONV_WIDTH = 31
D_FF = 4 * D_MODEL
REL_BUCKETS = 32
REL_MAX_DIST = 1024
ALPHA = (2 * DEPTH) ** 0.25
LN_EPS = 1e-5
RMS_EPS = 1e-6
NEG_INF = -1e30

SUBLANES = 8
HGRN_CHUNK = 128
ATT_QBLOCK = 128
ATT_LANES = 128
CONV_HALO = 16
CONV_ROWS = 32
CONV_LANES = 128
VMEM_LIMIT = 52 * 1024 * 1024


def _cparams(*sem):
    return pltpu.CompilerParams(dimension_semantics=sem, vmem_limit_bytes=VMEM_LIMIT)


def _dot(a, b):
    return jnp.dot(a, b, preferred_element_type=F32)


def _dot_nt(a, b):
    return lax.dot_general(a, b, (((1,), (1,)), ((), ())), preferred_element_type=F32)


def _dot_tn(a, b):
    return lax.dot_general(a, b, (((0,), (0,)), ((), ())), preferred_element_type=F32)


def _layer_norm(y, g, b):
    mu = jnp.mean(y, -1, keepdims=True)
    yc = y - mu
    var = jnp.mean(jnp.square(yc), -1, keepdims=True)
    return yc * lax.rsqrt(var + LN_EPS) * g + b


def _mm_body(x_ref, w_ref, o_ref):
    o_ref[...] = _dot(x_ref[...].astype(BF16), w_ref[...]).astype(o_ref.dtype)


def _matmul(x, w, out_dtype, tm, tn):
    m, k = x.shape
    n = w.shape[1]
    return pl.pallas_call(
        _mm_body,
        grid=(m // tm, n // tn),
        in_specs=[pl.BlockSpec((tm, k), lambda i, j: (i, 0)),
                  pl.BlockSpec((k, tn), lambda i, j: (0, j))],
        out_specs=pl.BlockSpec((tm, tn), lambda i, j: (i, j)),
        out_shape=jax.ShapeDtypeStruct((m, n), out_dtype),
        compiler_params=_cparams("parallel", "arbitrary"),
        name="in_proj",
    )(x, w)


def _hgrn_body(q_ref, v_ref, zf_ref, zb_ref, g_ref, lbp_ref, gn_ref, o_ref,
               qs, kf, kb, bf, cb, of, ob, sf, sb, *, seq, chunk, layer):
    C = chunk
    n_chunks = seq // C
    dk = A_HEAD_DIM
    row = lax.broadcasted_iota(jnp.int32, (C, dk), 0)

    def lower_bound(direction):
        p = [lbp_ref[direction, l:l + 1, :] for l in range(DEPTH + 1)]
        m = functools.reduce(jnp.maximum, p)
        e = [jnp.exp(v - m) for v in p]
        return sum(e[:layer + 1]) / sum(e)

    lb_f = lower_bound(0)
    lb_b = lower_bound(1)

    def gates(z, lb_d):
        logf = jnp.log(lb_d + (1.0 - lb_d) * jax.nn.sigmoid(z))
        k = (1.0 - lb_d) * jax.nn.sigmoid(-z)
        return k, logf

    def prep(n, carry):
        rows = pl.ds(pl.multiple_of(n * C, C), C)
        q = q_ref[0, rows, :]
        qs[rows, :] = q * jax.nn.sigmoid(q)
        k, lf = gates(zf_ref[0, rows, :], lb_f)
        kf[rows, :] = k
        s = 1
        while s < C:
            lf = lf + jnp.where(row >= s, pltpu.roll(lf, s, 0), 0.0)
            s *= 2
        bf[rows, :] = lf
        k, lg = gates(zb_ref[0, rows, :], lb_b)
        kb[rows, :] = k
        s = 1
        while s < C:
            lg = lg + jnp.where(row < C - s, pltpu.roll(lg, C - s, 0), 0.0)
            s *= 2
        cb[rows, :] = lg
        return carry

    lax.fori_loop(0, n_chunks, prep, 0)

    sf[...] = jnp.zeros_like(sf)
    sb[...] = jnp.zeros_like(sb)
    sub = lax.broadcasted_iota(jnp.int32, (C // SUBLANES, SUBLANES, dk), 1)
    trow = lax.broadcasted_iota(jnp.int32, (C, C), 0)
    tcol = lax.broadcasted_iota(jnp.int32, (C, C), 1)

    def intra(q, kfv, kbv, bfv, cbv, v):
        grp = (C // SUBLANES, SUBLANES, dk)
        acc = jnp.sum(q * (kfv + kbv), axis=-1, keepdims=True) * v
        q3, v3 = q.reshape(grp), v.reshape(grp)
        kf3, kb3, bf3, cb3 = kfv.reshape(grp), kbv.reshape(grp), bfv.reshape(grp), cbv.reshape(grp)
        acc3 = acc.reshape(grp)
        for d in range(1, SUBLANES):
            src_fwd = sub <= SUBLANES - 1 - d
            kr = pltpu.roll(jnp.where(src_fwd, kf3, kb3), d, 1)
            br = pltpu.roll(jnp.where(src_fwd, bf3, cb3), d, 1)
            vr = pltpu.roll(v3, d, 1)
            ex = jnp.exp(jnp.where(sub >= d, bf3, cb3) - br)
            a = jnp.sum(q3 * kr * ex, axis=-1, keepdims=True)
            acc3 = acc3 + a * vr
        acc = acc3.reshape(C, dk)
        scores = jnp.zeros((C, C), F32)
        w = SUBLANES
        while w < C:
            lg = w.bit_length() - 1
            nblk = C // (2 * w)
            rbf = jnp.concatenate(
                [jnp.broadcast_to(bfv[b * 2 * w + w:b * 2 * w + w + 1, :], (2 * w, dk)) for b in range(nblk)], 0)
            rcb = jnp.concatenate(
                [jnp.broadcast_to(cbv[b * 2 * w + w:b * 2 * w + w + 1, :], (2 * w, dk)) for b in range(nblk)], 0)
            second = ((row >> lg) & 1) == 1
            qt = q * jnp.exp(jnp.where(second, bfv - rbf, cbv - rcb))
            kt = jnp.where(second, kbv, kfv) * jnp.exp(jnp.where(second, rcb - cbv, rbf - bfv))
            p = _dot_nt(qt.astype(BF16), kt.astype(BF16))
            scores = scores + jnp.where(((trow >> lg) ^ (tcol >> lg)) == 1, p, 0.0)
            w *= 2
        return acc + _dot(scores.astype(BF16), v.astype(BF16))

    def step(n, carry):
        rows = pl.ds(pl.multiple_of(n * C, C), C)
        q, kfv, kbv, bfv, cbv, v = qs[rows, :], kf[rows, :], kb[rows, :], bf[rows, :], cb[rows, :], v_ref[0, rows, :]
        o = intra(q, kfv, kbv, bfv, cbv, v)
        st = sf[...]
        o = o + _dot_nt((q * jnp.exp(bfv)).astype(BF16), st.astype(BF16))
        of[rows, :] = o
        b_last = bfv[C - 1:C, :]
        kt = (kfv * jnp.exp(b_last - bfv)).astype(BF16)
        sf[...] = jnp.exp(b_last) * st + _dot_tn(v.astype(BF16), kt)
        rows = pl.ds(pl.multiple_of((n_chunks - 1 - n) * C, C), C)
        q, kbv, cbv, v = qs[rows, :], kb[rows, :], cb[rows, :], v_ref[0, rows, :]
        st = sb[...]
        ob[rows, :] = _dot_nt((q * jnp.exp(cbv)).astype(BF16), st.astype(BF16))
        c_first = cbv[0:1, :]
        kt = (kbv * jnp.exp(c_first - cbv)).astype(BF16)
        sb[...] = jnp.exp(c_first) * st + _dot_tn(v.astype(BF16), kt)
        return carry

    lax.fori_loop(0, n_chunks, step, 0)

    gn = gn_ref[...]

    def finish(n, carry):
        rows = pl.ds(pl.multiple_of(n * C, C), C)
        o = of[rows, :] + ob[rows, :]
        o = o * lax.rsqrt(jnp.mean(o * o, -1, keepdims=True) + RMS_EPS)
        g = g_ref[0, rows, :]
        o_ref[0, rows, :] = (o * gn * (g * jax.nn.sigmoid(g))).astype(o_ref.dtype)
        return carry

    lax.fori_loop(0, n_chunks, finish, 0)


def _hgrn_mixer(ha, hgrn_lb, g_norm, layer):
    bsz, seq, _ = ha.shape
    dk = A_HEAD_DIM
    chunk = min(HGRN_CHUNK, seq)

    def col(part):
        return pl.BlockSpec((1, seq, dk), lambda b, h, part=part: (b, 0, part * A_HEADS + h))

    body = functools.partial(_hgrn_body, seq=seq, chunk=chunk, layer=layer)
    return pl.pallas_call(
        body,
        grid=(bsz, A_HEADS),
        in_specs=[col(0), col(1), col(2), col(3), col(4),
                  pl.BlockSpec((2, DEPTH + 1, dk), lambda b, h: (0, 0, h)),
                  pl.BlockSpec((1, dk), lambda b, h: (0, h))],
        out_specs=pl.BlockSpec((1, seq, dk), lambda b, h: (b, 0, h)),
        out_shape=jax.ShapeDtypeStruct((bsz, seq, A_WIDTH), BF16),
        scratch_shapes=[pltpu.VMEM((seq, dk), F32)] * 7 + [pltpu.VMEM((dk, dk), F32)] * 2,
        compiler_params=_cparams("parallel", "arbitrary"),
        name="hgrn2",
    )(ha, ha, ha, ha, ha, hgrn_lb, g_norm.reshape(1, A_WIDTH))


def _t5_buckets(rel):
    half = REL_BUCKETS // 2
    max_exact = half // 2
    n = np.abs(rel)
    large = max_exact + (np.log(np.maximum(n, 1) / max_exact) / np.log(REL_MAX_DIST / max_exact)
                         * (half - max_exact)).astype(np.int32)
    large = np.minimum(large, half - 1)
    return (np.where(rel > 0, half, 0) + np.where(n < max_exact, n, large)).astype(np.int32)


def _att_geometry(sub_len, half):
    lq = min(ATT_QBLOCK, sub_len)
    win = min(lq + 2 * half, sub_len)
    return lq, win, sub_len // lq


def _att_bias(rel_bias_g, sub_len, dilation, half):
    lq, win, _ = _att_geometry(sub_len, half)
    p = win + lq - 1
    tables = []
    for delta in (0, -half, lq - win):
        rel = np.arange(p + 1) - (lq - 1) + delta
        valid = (np.abs(rel) <= half) & (np.arange(p + 1) < p)
        u = jnp.where(valid[:, None], rel_bias_g[_t5_buckets(rel * dilation)], NEG_INF).T
        flat = jnp.tile(u, (1, lq))[:, :lq * p]
        tables.append(flat.reshape(-1, lq, p)[:, :, lq - 1:lq - 1 + win])
    return jnp.stack(tables).astype(F32)


def _att_body(q_ref, k_ref, v_ref, bias_ref, o_ref, lse_ref, qc, kc, vc, oacc, *, seq, r, lq, win, half):
    sub_len = seq // r
    nq = sub_len // lq
    lanes = q_ref.shape[-1]
    lane = lax.broadcasted_iota(jnp.int32, (lq, lanes), 1)

    def residue(c, carry):
        members = pl.ds(c, sub_len, stride=r)
        qc[...] = (q_ref[members, :] * (B_HEAD_DIM ** -0.5)).astype(BF16)
        kc[...] = k_ref[members, :].astype(BF16)
        vc[...] = v_ref[members, :].astype(BF16)

        def qblock(iq, carry2):
            q0 = pl.multiple_of(iq * lq, lq)
            ws = pl.multiple_of(jnp.clip(q0 - half, 0, sub_len - win), half)
            case = jnp.where(iq == 0, 0, jnp.where(iq == nq - 1, 2, 1))
            q = qc[pl.ds(q0, lq), :]
            kwin = kc[pl.ds(ws, win), :]
            vwin = vc[pl.ds(ws, win), :]
            out = jnp.zeros((lq, lanes), F32)
            lse = jnp.zeros((lq, lanes), F32)
            for h in range(lanes // B_HEAD_DIM):
                mine = (lane >= h * B_HEAD_DIM) & (lane < (h + 1) * B_HEAD_DIM)
                s = _dot_nt(jnp.where(mine, q, jnp.zeros_like(q)), kwin) + bias_ref[case, h]
                m = jnp.max(s, -1, keepdims=True)
                p = jnp.exp(s - m)
                den = jnp.sum(p, -1, keepdims=True)
                o = _dot(p.astype(BF16), vwin) / den
                out = jnp.where(mine, o, out)
                lse = jnp.where(mine, m + jnp.log(den), lse)
            dst = pl.ds(c + q0 * r, lq, stride=r)
            oacc[dst, :] = out
            lse_ref[dst, :] = lse
            return carry2

        lax.fori_loop(0, nq, qblock, 0)
        return carry

    lax.fori_loop(0, r, residue, 0)
    o_ref[...] = oacc[...].astype(o_ref.dtype)


def _dilated_group(hb, rel_bias_g, gi, window, dilation):
    bsz, seq, _ = hb.shape
    r = dilation
    sub_len = seq // r
    half = window // (2 * r)
    lq, win, nq = _att_geometry(sub_len, half)
    lanes = ATT_LANES
    pairs = B_GROUP_WIDTH // lanes
    heads = lanes // B_HEAD_DIM
    bias = _att_bias(rel_bias_g, sub_len, r, half)

    def part(p):
        return pl.BlockSpec((None, seq, lanes), lambda b, hp, p=p: (b, 0, (p * len(B_GROUPS) + gi) * pairs + hp))

    def out_spec():
        return pl.BlockSpec((None, seq, lanes), lambda b, hp: (b, 0, hp))

    body = functools.partial(_att_body, seq=seq, r=r, lq=lq, win=win, half=half)
    return pl.pallas_call(
        body,
        grid=(bsz, pairs),
        in_specs=[part(0), part(1), part(2),
                  pl.BlockSpec((3, heads, lq, win), lambda b, hp: (0, hp, 0, 0))],
        out_specs=[out_spec(), out_spec()],
        out_shape=[jax.ShapeDtypeStruct((bsz, seq, B_GROUP_WIDTH), BF16),
                   jax.ShapeDtypeStruct((bsz, seq, B_GROUP_WIDTH), F32)],
        scratch_shapes=[pltpu.VMEM((sub_len, lanes), BF16)] * 3 + [pltpu.VMEM((seq, lanes), F32)],
        compiler_params=_cparams("parallel", "arbitrary"),
        name=f"dilated_attn_r{r}",
    )(hb, hb, hb, bias)


def _ab_out_body(oa_ref, o1_ref, o2_ref, o3_ref, l1_ref, l2_ref, l3_ref, x_ref, w_ref, g_ref, b_ref, y_ref):
    l1, l2, l3 = l1_ref[...], l2_ref[...], l3_ref[...]
    m = jnp.maximum(jnp.maximum(l1, l2), l3)
    e1, e2, e3 = jnp.exp(l1 - m), jnp.exp(l2 - m), jnp.exp(l3 - m)
    tot = e1 + e2 + e3
    mix = _dot(oa_ref[...], w_ref[0:A_WIDTH, :])
    for gi, (o_ref, e) in enumerate(((o1_ref, e1), (o2_ref, e2), (o3_ref, e3))):
        og = (o_ref[...].astype(F32) * (e / tot)).astype(BF16)
        lo = A_WIDTH + gi * B_GROUP_WIDTH
        mix = mix + _dot(og, w_ref[lo:lo + B_GROUP_WIDTH, :])
    y_ref[...] = _layer_norm(ALPHA * x_ref[...] + mix, g_ref[...], b_ref[...])


def _ab_out(oa, outs, lses, x, w_out, ln_g, ln_b, tm):
    t = x.shape[0]
    gw = B_GROUP_WIDTH

    def rowblock(width):
        return pl.BlockSpec((tm, width), lambda i: (i, 0))

    def whole(shape):
        return pl.BlockSpec(shape, lambda i: (0, 0))

    return pl.pallas_call(
        _ab_out_body,
        grid=(t // tm,),
        in_specs=[rowblock(A_WIDTH)] + [rowblock(gw)] * 6 + [rowblock(D_MODEL), whole(w_out.shape),
                                                           whole((1, D_MODEL)), whole((1, D_MODEL))],
        out_specs=rowblock(D_MODEL),
        out_shape=jax.ShapeDtypeStruct((t, D_MODEL), F32),
        compiler_params=_cparams("parallel"),
        name="ab_out_ln",
    )(oa, *outs, *lses, x, w_out, ln_g.reshape(1, D_MODEL), ln_b.reshape(1, D_MODEL))


def _mlp_body(x_ref, w1_ref, w2_ref, g_ref, b_ref, y_ref, xb, acc):
    j = pl.program_id(1)

    @pl.when(j == 0)
    def _():
        xb[...] = x_ref[...].astype(BF16)
        acc[...] = jnp.zeros_like(acc)

    h = jnp.square(jnp.maximum(_dot(xb[...], w1_ref[...]), 0.0))
    acc[...] += _dot(h.astype(BF16), w2_ref[...])

    @pl.when(j == pl.num_programs(1) - 1)
    def _():
        y_ref[...] = _layer_norm(ALPHA * x_ref[...] + acc[...], g_ref[...], b_ref[...])


def _mlp(x, w1, w2, ln_g, ln_b, tm, tf):
    t = x.shape[0]
    return pl.pallas_call(
        _mlp_body,
        grid=(t // tm, D_FF // tf),
        in_specs=[pl.BlockSpec((tm, D_MODEL), lambda i, j: (i, 0)),
                  pl.BlockSpec((D_MODEL, tf), lambda i, j: (0, j)),
                  pl.BlockSpec((tf, D_MODEL), lambda i, j: (j, 0)),
                  pl.BlockSpec((1, D_MODEL), lambda i, j: (0, 0)),
                  pl.BlockSpec((1, D_MODEL), lambda i, j: (0, 0))],
        out_specs=pl.BlockSpec((tm, D_MODEL), lambda i, j: (i, 0)),
        out_shape=jax.ShapeDtypeStruct((t, D_MODEL), F32),
        scratch_shapes=[pltpu.VMEM((tm, D_MODEL), BF16), pltpu.VMEM((tm, D_MODEL), F32)],
        compiler_params=_cparams("parallel", "arbitrary"),
        name="mlp_ln",
    )(x, w1, w2, ln_g.reshape(1, D_MODEL), ln_b.reshape(1, D_MODEL))


def _glu_body(x_ref, wa_ref, wg_ref, ba_ref, bg_ref, u_ref):
    xb = x_ref[...].astype(BF16)
    a = _dot(xb, wa_ref[...]) + ba_ref[...]
    gate = _dot(xb, wg_ref[...]) + bg_ref[...]
    u_ref[...] = a * jax.nn.sigmoid(gate)


def _glu_proj(x, w_in, b_in, tm, tn):
    t = x.shape[0]
    nb = D_MODEL // tn
    b2 = b_in.reshape(1, 2 * D_MODEL)
    return pl.pallas_call(
        _glu_body,
        grid=(t // tm, nb),
        in_specs=[pl.BlockSpec((tm, D_MODEL), lambda i, j: (i, 0)),
                  pl.BlockSpec((D_MODEL, tn), lambda i, j: (0, j)),
                  pl.BlockSpec((D_MODEL, tn), lambda i, j: (0, j + nb)),
                  pl.BlockSpec((1, tn), lambda i, j: (0, j)),
                  pl.BlockSpec((1, tn), lambda i, j: (0, j + nb))],
        out_specs=pl.BlockSpec((tm, tn), lambda i, j: (i, j)),
        out_shape=jax.ShapeDtypeStruct((t, D_MODEL), F32),
        compiler_params=_cparams("parallel", "arbitrary"),
        name="conv_glu",
    )(x, w_in, w_in, b2, b2)


def _conv_out_body(u_ref, up_ref, un_ref, x_ref, dw_ref, dwb_ref, ng_ref, nb_ref, w_ref, bo_ref, g_ref, b_ref,
                   y_ref, buf, act, *, ts):
    i = pl.program_id(1)
    halo = CONV_HALO
    pad = CONV_WIDTH // 2
    for s in range(D_MODEL // CONV_LANES):
        cs = slice(s * CONV_LANES, (s + 1) * CONV_LANES)
        buf[s, 0:halo, :] = jnp.where(i > 0, up_ref[0, :, cs], 0.0)
        buf[s, halo:halo + ts, :] = u_ref[0, :, cs]
        buf[s, halo + ts:halo + ts + halo, :] = jnp.where(i < pl.num_programs(1) - 1, un_ref[0, :, cs], 0.0)
    dwb, ng, nb = dwb_ref[...], ng_ref[...], nb_ref[...]

    def tile(rt, carry):
        base = pl.multiple_of(rt * CONV_ROWS, CONV_ROWS)
        cols = []
        for s in range(D_MODEL // CONV_LANES):
            acc = jnp.zeros((CONV_ROWS, CONV_LANES), F32)
            for j in range(CONV_WIDTH):
                rows = pl.ds(base + (halo - pad + j), CONV_ROWS, stride=1)
                acc = acc + dw_ref[j:j + 1, s * CONV_LANES:(s + 1) * CONV_LANES] * buf[s, rows, :]
            cols.append(acc)
        c = _layer_norm(jnp.concatenate(cols, axis=1) + dwb, ng, nb)
        act[pl.ds(base, CONV_ROWS), :] = (c * jax.nn.sigmoid(c)).astype(BF16)
        return carry

    lax.fori_loop(0, ts // CONV_ROWS, tile, 0)
    mix = _dot(act[...], w_ref[...]) + bo_ref[...]
    y_ref[0] = _layer_norm(ALPHA * x_ref[0] + mix, g_ref[...], b_ref[...])


def _conv_out(u, x, dw, dw_b, n_g, n_b, w_out, b_out, ln_g, ln_b, ts):
    bsz, seq, d = x.shape
    hb = ts // CONV_HALO
    last_halo = seq // CONV_HALO - 1

    def vec():
        return pl.BlockSpec((1, d), lambda b, i: (0, 0))

    body = functools.partial(_conv_out_body, ts=ts)
    return pl.pallas_call(
        body,
        grid=(bsz, seq // ts),
        in_specs=[pl.BlockSpec((1, ts, d), lambda b, i: (b, i, 0)),
                  pl.BlockSpec((1, CONV_HALO, d), lambda b, i: (b, jnp.maximum(i * hb - 1, 0), 0)),
                  pl.BlockSpec((1, CONV_HALO, d), lambda b, i: (b, jnp.minimum((i + 1) * hb, last_halo), 0)),
                  pl.BlockSpec((1, ts, d), lambda b, i: (b, i, 0)),
                  pl.BlockSpec((CONV_WIDTH, d), lambda b, i: (0, 0)),
                  vec(), vec(), vec(),
                  pl.BlockSpec((d, d), lambda b, i: (0, 0)),
                  vec(), vec(), vec()],
        out_specs=pl.BlockSpec((1, ts, d), lambda b, i: (b, i, 0)),
        out_shape=jax.ShapeDtypeStruct((bsz, seq, d), F32),
        scratch_shapes=[pltpu.VMEM((d // CONV_LANES, ts + 2 * CONV_HALO, CONV_LANES), F32),
                        pltpu.VMEM((ts, d), BF16)],
        compiler_params=_cparams("parallel", "arbitrary"),
        name="conv_out_ln",
    )(u, u, u, x, dw, dw_b.reshape(1, d), n_g.reshape(1, d), n_b.reshape(1, d), w_out,
      b_out.reshape(1, d), ln_g.reshape(1, d), ln_b.reshape(1, d))


def _tiles(tokens):
    tm = 1024 if tokens % 1024 == 0 else 256
    return tm


def _trunk(x, rel_bias, hgrn_lb, w_in_ab, hgrn_norm, w_out_ab, w_in_c, b_in_c, dw_c, dw_b_c, cnorm_g, cnorm_b,
           w_out_c, b_out_c, ln_mix_g, ln_mix_b, mlp_w1, mlp_w2, ln_ffn_g, ln_ffn_b):
    bsz, seq, d = x.shape
    t = bsz * seq
    tm = _tiles(t)
    x2 = x.reshape(t, d)
    for l in range(DEPTH):
        i = l // 2
        if l % 2 == 0:
            wa = w_in_ab[i][:, :5 * A_WIDTH].astype(BF16)
            wb = w_in_ab[i][:, 5 * A_WIDTH:].astype(BF16)
            ha = _matmul(x2, wa, F32, tm, 5 * A_WIDTH // 2)
            hb = _matmul(x2, wb, F32, tm, 3 * B_WIDTH // 2)
            oa = _hgrn_mixer(ha.reshape(bsz, seq, 5 * A_WIDTH), hgrn_lb, hgrn_norm[i], l)
            hb3 = hb.reshape(bsz, seq, 3 * B_WIDTH)
            outs, lses = [], []
            for gi, (window, r) in enumerate(B_GROUPS):
                hs = slice(gi * B_HEADS_PER_GROUP, (gi + 1) * B_HEADS_PER_GROUP)
                o, lse = _dilated_group(hb3, rel_bias[:, hs], gi, window, r)
                outs.append(o.reshape(t, B_GROUP_WIDTH))
                lses.append(lse.reshape(t, B_GROUP_WIDTH))
            x2 = _ab_out(oa.reshape(t, A_WIDTH), outs, lses, x2, w_out_ab[i].astype(BF16),
                         ln_mix_g[l], ln_mix_b[l], tm)
        else:
            u = _glu_proj(x2, w_in_c[i].astype(BF16), b_in_c[i], tm, D_MODEL // 2)
            x2 = _conv_out(u.reshape(bsz, seq, d), x2.reshape(bsz, seq, d), dw_c[i], dw_b_c[i], cnorm_g[i],
                           cnorm_b[i], w_out_c[i].astype(BF16), b_out_c[i], ln_mix_g[l], ln_mix_b[l],
                           min(512, seq)).reshape(t, d)
        x2 = _mlp(x2, mlp_w1[l].astype(BF16), mlp_w2[l].astype(BF16), ln_ffn_g[l], ln_ffn_b[l], tm, 1024)
    return x2.reshape(bsz, seq, d)


def kernel(x_prompt, x_sample, rel_bias, hgrn_lb, w_in_ab, hgrn_norm, w_out_ab, w_in_c, b_in_c, dw_c, dw_b_c,
           cnorm_g, cnorm_b, w_out_c, b_out_c, ln_mix_g, ln_mix_b, mlp_w1, mlp_w2, ln_ffn_g, ln_ffn_b):
    params = (rel_bias, hgrn_lb, w_in_ab, hgrn_norm, w_out_ab, w_in_c, b_in_c, dw_c, dw_b_c, cnorm_g, cnorm_b,
              w_out_c, b_out_c, ln_mix_g, ln_mix_b, mlp_w1, mlp_w2, ln_ffn_g, ln_ffn_b)
    return (_trunk(x_prompt, *params), _trunk(x_sample, *params))
```

```python
import functools

import numpy as np
import jax
import jax.numpy as jnp
from jax import lax
from jax.experimental import pallas as pl
from jax.experimental.pallas import tpu as pltpu

F32 = jnp.float32
BF16 = jnp.bfloat16

D_MODEL = 1024
DEPTH = 2
A_HEADS = 4
A_HEAD_DIM = 128
A_WIDTH = A_HEADS * A_HEAD_DIM
B_GROUPS = ((128, 1), (512, 4), (2048, 16))
B_HEADS_PER_GROUP = 4
B_HEAD_DIM = 64
B_GROUP_WIDTH = B_HEADS_PER_GROUP * B_HEAD_DIM
B_WIDTH = B_GROUP_WIDTH * len(B_GROUPS)
CONV_WIDTH = 31
D_FF = 4 * D_MODEL
REL_BUCKETS = 32
REL_MAX_DIST = 1024
ALPHA = (2 * DEPTH) ** 0.25
LN_EPS = 1e-5
RMS_EPS = 1e-6
NEG_INF = -1e30

SUBLANES = 8
IN_PROJ_ROWS = 256
HGRN_CHUNK = 128
ATT_QBLOCK = 128
ATT_LANES = 128
ATT_UNROLL = 4
CONV_HALO = 16
CONV_ROWS = 32
CONV_LANES = 128
VMEM_LIMIT = 52 * 1024 * 1024


def _cparams(*sem):
    return pltpu.CompilerParams(dimension_semantics=sem, vmem_limit_bytes=VMEM_LIMIT)


def _dot(a, b):
    return jnp.dot(a, b, preferred_element_type=F32)


def _dot_nt(a, b):
    return lax.dot_general(a, b, (((1,), (1,)), ((), ())), preferred_element_type=F32)


def _dot_tn(a, b):
    return lax.dot_general(a, b, (((0,), (0,)), ((), ())), preferred_element_type=F32)


def _layer_norm(y, g, b):
    mu = jnp.mean(y, -1, keepdims=True)
    yc = y - mu
    var = jnp.mean(jnp.square(yc), -1, keepdims=True)
    return yc * lax.rsqrt(var + LN_EPS) * g + b


def _in_proj_body(x_ref, wa_ref, wb_ref, ha_ref, hb_ref):
    xb = x_ref[...].astype(BF16)
    ha_ref[...] = _dot(xb, wa_ref[...])
    hb_ref[...] = _dot(xb, wb_ref[...])


def _in_proj(x, wa, wb, tm):
    m, k = x.shape
    na, nb = wa.shape[1], wb.shape[1]
    return pl.pallas_call(
        _in_proj_body,
        grid=(m // tm,),
        in_specs=[pl.BlockSpec((tm, k), lambda i: (i, 0)),
                  pl.BlockSpec((k, na), lambda i: (0, 0)),
                  pl.BlockSpec((k, nb), lambda i: (0, 0))],
        out_specs=[pl.BlockSpec((tm, na), lambda i: (i, 0)),
                   pl.BlockSpec((tm, nb), lambda i: (i, 0))],
        out_shape=[jax.ShapeDtypeStruct((m, na), F32), jax.ShapeDtypeStruct((m, nb), F32)],
        compiler_params=_cparams("parallel"),
        name="in_proj",
    )(x, wa, wb)


def _hgrn_body(q_ref, v_ref, zf_ref, zb_ref, g_ref, lbp_ref, gn_ref, o_ref,
               qs, kf, kb, bf, cb, of, ob, sf, sb, level, *, seq, chunk, layer):
    C = chunk
    n_chunks = seq // C
    dk = A_HEAD_DIM
    grp = (C // SUBLANES, SUBLANES, dk)
    row = lax.broadcasted_iota(jnp.int32, (C, dk), 0)

    def lower_bound(direction):
        p = [lbp_ref[direction, l:l + 1, :] for l in range(DEPTH + 1)]
        m = functools.reduce(jnp.maximum, p)
        e = [jnp.exp(v - m) for v in p]
        return sum(e[:layer + 1]) / sum(e)

    lb_f = lower_bound(0)
    lb_b = lower_bound(1)

    def gates(z, lb_d):
        log2f = jnp.log2(lb_d + (1.0 - lb_d) * jax.nn.sigmoid(z))
        k = (1.0 - lb_d) * jax.nn.sigmoid(-z)
        return k, log2f

    def prep(n, carry):
        rows = pl.ds(pl.multiple_of(n * C, C), C)
        q = q_ref[0, rows, :]
        qs[rows, :] = q * jax.nn.sigmoid(q)
        k, lf = gates(zf_ref[0, rows, :], lb_f)
        kf[rows, :] = k
        s = 1
        while s < C:
            lf = lf + jnp.where(row >= s, pltpu.roll(lf, s, 0), 0.0)
            s *= 2
        bf[rows, :] = lf
        k, lg = gates(zb_ref[0, rows, :], lb_b)
        kb[rows, :] = k
        s = 1
        while s < C:
            lg = lg + jnp.where(row < C - s, pltpu.roll(lg, C - s, 0), 0.0)
            s *= 2
        cb[rows, :] = lg
        return carry

    lax.fori_loop(0, n_chunks, prep, 0)

    sf[...] = jnp.zeros_like(sf)
    sb[...] = jnp.zeros_like(sb)
    sub = lax.broadcasted_iota(jnp.int32, grp, 1)
    level[...] = 31 - lax.clz(lax.broadcasted_iota(jnp.int32, (C, C), 0) ^ lax.broadcasted_iota(jnp.int32, (C, C), 1))

    def mid_row(x3, w):
        if w == 1:
            return jnp.where((sub & 1) == 1, x3, pltpu.roll(x3, SUBLANES - 1, 1))
        if w == 2:
            return jnp.where(sub < 4, jnp.broadcast_to(x3[:, 2:3, :], grp), jnp.broadcast_to(x3[:, 6:7, :], grp))
        return jnp.broadcast_to(x3[:, 4:5, :], grp)

    def intra(q, kfv, kbv, bfv, cbv, v):
        lv = level[...]
        scores = jnp.zeros((C, C), F32)
        q3 = q.reshape(grp)
        kf3, kb3, bf3, cb3 = kfv.reshape(grp), kbv.reshape(grp), bfv.reshape(grp), cbv.reshape(grp)
        w = 1
        while w < C:
            if w < SUBLANES:
                second = (sub & w) != 0
                d_f = bf3 - mid_row(bf3, w)
                d_b = cb3 - mid_row(cb3, w)
                qt = (q3 * jnp.exp2(jnp.where(second, d_f, d_b))).reshape(C, dk)
                kt = (jnp.where(second, kb3, kf3) * jnp.exp2(-jnp.where(second, d_b, d_f))).reshape(C, dk)
            else:
                qparts, kparts = [], []
                for r0 in range(0, C, 2 * w):
                    first, second = slice(r0, r0 + w), slice(r0 + w, r0 + 2 * w)
                    bf_r, cb_r = bfv[r0 + w:r0 + w + 1, :], cbv[r0 + w:r0 + w + 1, :]
                    qparts += [q[first] * jnp.exp2(cbv[first] - cb_r), q[second] * jnp.exp2(bfv[second] - bf_r)]
                    kparts += [kfv[first] * jnp.exp2(bf_r - bfv[first]), kbv[second] * jnp.exp2(cb_r - cbv[second])]
                qt, kt = jnp.concatenate(qparts, 0), jnp.concatenate(kparts, 0)
            p = _dot_nt(qt.astype(BF16), kt.astype(BF16))
            scores = jnp.where(lv == w.bit_length() - 1, p, scores)
            w *= 2
        same = jnp.sum(q * (kfv + kbv), axis=-1, keepdims=True) * v
        return same + _dot(scores.astype(BF16), v.astype(BF16))

    def step(n, carry):
        rows = pl.ds(pl.multiple_of(n * C, C), C)
        q, kfv, kbv, bfv, cbv, v = qs[rows, :], kf[rows, :], kb[rows, :], bf[rows, :], cb[rows, :], v_ref[0, rows, :]
        o = intra(q, kfv, kbv, bfv, cbv, v)
        st = sf[...]
        o = o + _dot_nt((q * jnp.exp2(bfv)).astype(BF16), st.astype(BF16))
        of[rows, :] = o
        b_last = bfv[C - 1:C, :]
        kt = (kfv * jnp.exp2(b_last - bfv)).astype(BF16)
        sf[...] = jnp.exp2(b_last) * st + _dot_tn(v.astype(BF16), kt)
        rows = pl.ds(pl.multiple_of((n_chunks - 1 - n) * C, C), C)
        q, kbv, cbv, v = qs[rows, :], kb[rows, :], cb[rows, :], v_ref[0, rows, :]
        st = sb[...]
        ob[rows, :] = _dot_nt((q * jnp.exp2(cbv)).astype(BF16), st.astype(BF16))
        c_first = cbv[0:1, :]
        kt = (kbv * jnp.exp2(c_first - cbv)).astype(BF16)
        sb[...] = jnp.exp2(c_first) * st + _dot_tn(v.astype(BF16), kt)
        return carry

    lax.fori_loop(0, n_chunks, step, 0, unroll=2 if n_chunks % 2 == 0 else 1)

    gn = gn_ref[...]

    def finish(n, carry):
        rows = pl.ds(pl.multiple_of(n * C, C), C)
        o = of[rows, :] + ob[rows, :]
        o = o * lax.rsqrt(jnp.mean(o * o, -1, keepdims=True) + RMS_EPS)
        g = g_ref[0, rows, :]
        o_ref[0, rows, :] = (o * gn * (g * jax.nn.sigmoid(g))).astype(o_ref.dtype)
        return carry

    lax.fori_loop(0, n_chunks, finish, 0)


def _hgrn_mixer(ha, hgrn_lb, g_norm, layer):
    bsz, seq, _ = ha.shape
    dk = A_HEAD_DIM
    chunk = min(HGRN_CHUNK, seq)

    def col(part):
        return pl.BlockSpec((1, seq, dk), lambda b, h, part=part: (b, 0, part * A_HEADS + h))

    body = functools.partial(_hgrn_body, seq=seq, chunk=chunk, layer=layer)
    return pl.pallas_call(
        body,
        grid=(bsz, A_HEADS),
        in_specs=[col(0), col(1), col(2), col(3), col(4),
                  pl.BlockSpec((2, DEPTH + 1, dk), lambda b, h: (0, 0, h)),
                  pl.BlockSpec((1, dk), lambda b, h: (0, h))],
        out_specs=pl.BlockSpec((1, seq, dk), lambda b, h: (b, 0, h)),
        out_shape=jax.ShapeDtypeStruct((bsz, seq, A_WIDTH), BF16),
        scratch_shapes=[pltpu.VMEM((seq, dk), F32)] * 7 + [pltpu.VMEM((dk, dk), F32)] * 2
        + [pltpu.VMEM((chunk, chunk), jnp.int32)],
        compiler_params=_cparams("parallel", "arbitrary"),
        name="hgrn2",
    )(ha, ha, ha, ha, ha, hgrn_lb, g_norm.reshape(1, A_WIDTH))


def _t5_buckets(rel):
    half = REL_BUCKETS // 2
    max_exact = half // 2
    n = np.abs(rel)
    large = max_exact + (np.log(np.maximum(n, 1) / max_exact) / np.log(REL_MAX_DIST / max_exact)
                         * (half - max_exact)).astype(np.int32)
    large = np.minimum(large, half - 1)
    return (np.where(rel > 0, half, 0) + np.where(n < max_exact, n, large)).astype(np.int32)


def _att_geometry(sub_len, half):
    lq = min(ATT_QBLOCK, sub_len)
    win = min(lq + 2 * half, sub_len)
    return lq, win, sub_len // lq


def _att_bias(rel_bias_g, sub_len, dilation, half):
    lq, win, _ = _att_geometry(sub_len, half)
    p = win + lq - 1
    tables = []
    for delta in (0, -half, lq - win):
        rel = np.arange(p + 1) - (lq - 1) + delta
        valid = (np.abs(rel) <= half) & (np.arange(p + 1) < p)
        u = jnp.where(valid[:, None], rel_bias_g[_t5_buckets(rel * dilation)], NEG_INF).T
        flat = jnp.tile(u, (1, lq))[:, :lq * p]
        tables.append(flat.reshape(-1, lq, p)[:, :, lq - 1:lq - 1 + win])
    return jnp.stack(tables).astype(F32)


def _att_body(q_ref, k_ref, v_ref, bias_ref, o_ref, lse_ref, qd, kd, vd, oacc, *, seq, r, lq, win, half):
    sub_len = seq // r
    nq = sub_len // lq
    n_tasks = r * nq
    unroll = min(ATT_UNROLL, n_tasks)
    lanes = q_ref.shape[-1]
    heads = lanes // B_HEAD_DIM
    lane = lax.broadcasted_iota(jnp.int32, (lq, lanes), 1)
    head_lanes = [(lane >= h * B_HEAD_DIM) & (lane < (h + 1) * B_HEAD_DIM) for h in range(heads)]

    def split(c, carry):
        members = pl.ds(c, sub_len, stride=r)
        dst = pl.ds(pl.multiple_of(c * sub_len, sub_len), sub_len)
        qd[dst, :] = (q_ref[members, :] * (B_HEAD_DIM ** -0.5)).astype(BF16)
        kd[dst, :] = k_ref[members, :].astype(BF16)
        vd[dst, :] = v_ref[members, :].astype(BF16)
        return carry

    lax.fori_loop(0, r, split, 0)

    def task(tk):
        c = tk // nq
        iq = tk % nq
        q0 = iq * lq
        ws = jnp.clip(q0 - half, 0, sub_len - win)
        case = jnp.where(iq == 0, 0, jnp.where(iq == nq - 1, 2, 1))
        base = c * sub_len
        q = qd[pl.ds(pl.multiple_of(base + q0, lq), lq), :]
        kwin = kd[pl.ds(pl.multiple_of(base + ws, half), win), :]
        vwin = vd[pl.ds(pl.multiple_of(base + ws, half), win), :]
        qm = jnp.concatenate([jnp.where(hl, q, jnp.zeros_like(q)) for hl in head_lanes], axis=0)
        bias = jnp.concatenate([bias_ref[case, h] for h in range(heads)], axis=0)
        s = _dot_nt(qm, kwin) + bias
        m = jnp.max(s, -1, keepdims=True)
        p = jnp.exp(s - m)
        den = jnp.sum(p, -1, keepdims=True)
        o2 = _dot(p.astype(BF16), vwin) / den
        lse2 = m + jnp.log(den)
        out = o2[0:lq]
        lse = jnp.broadcast_to(lse2[0:lq], (lq, lanes))
        for h in range(1, heads):
            out = jnp.where(head_lanes[h], o2[h * lq:(h + 1) * lq], out)
            lse = jnp.where(head_lanes[h], lse2[h * lq:(h + 1) * lq], lse)
        dst = pl.ds(c + q0 * r, lq, stride=r)
        oacc[dst, :] = out
        lse_ref[dst, :] = lse

    def group(i, carry):
        for u in range(unroll):
            task(i * unroll + u)
        return carry

    lax.fori_loop(0, n_tasks // unroll, group, 0)
    o_ref[...] = oacc[...].astype(o_ref.dtype)


def _dilated_group(hb, rel_bias_g, gi, window, dilation):
    bsz, seq, _ = hb.shape
    r = dilation
    sub_len = seq // r
    half = window // (2 * r)
    lq, win, nq = _att_geometry(sub_len, half)
    lanes = ATT_LANES
    pairs = B_GROUP_WIDTH // lanes
    heads = lanes // B_HEAD_DIM
    bias = _att_bias(rel_bias_g, sub_len, r, half)

    def part(p):
        return pl.BlockSpec((None, seq, lanes), lambda b, hp, p=p: (b, 0, (p * len(B_GROUPS) + gi) * pairs + hp))

    def out_spec():
        return pl.BlockSpec((None, seq, lanes), lambda b, hp: (b, 0, hp))

    body = functools.partial(_att_body, seq=seq, r=r, lq=lq, win=win, half=half)
    return pl.pallas_call(
        body,
        grid=(bsz, pairs),
        in_specs=[part(0), part(1), part(2),
                  pl.BlockSpec((3, heads, lq, win), lambda b, hp: (0, hp, 0, 0))],
        out_specs=[out_spec(), out_spec()],
        out_shape=[jax.ShapeDtypeStruct((bsz, seq, B_GROUP_WIDTH), BF16),
                   jax.ShapeDtypeStruct((bsz, seq, B_GROUP_WIDTH), F32)],
        scratch_shapes=[pltpu.VMEM((seq, lanes), BF16)] * 3 + [pltpu.VMEM((seq, lanes), F32)],
        compiler_params=_cparams("parallel", "arbitrary"),
        name=f"dilated_attn_r{r}",
    )(hb, hb, hb, bias)


def _ab_out_body(oa_ref, o1_ref, o2_ref, o3_ref, l1_ref, l2_ref, l3_ref, x_ref, w_ref, g_ref, b_ref, y_ref):
    l1, l2, l3 = l1_ref[...], l2_ref[...], l3_ref[...]
    m = jnp.maximum(jnp.maximum(l1, l2), l3)
    e1, e2, e3 = jnp.exp(l1 - m), jnp.exp(l2 - m), jnp.exp(l3 - m)
    tot = e1 + e2 + e3
    mix = _dot(oa_ref[...], w_ref[0:A_WIDTH, :])
    for gi, (o_ref, e) in enumerate(((o1_ref, e1), (o2_ref, e2), (o3_ref, e3))):
        og = (o_ref[...].astype(F32) * (e / tot)).astype(BF16)
        lo = A_WIDTH + gi * B_GROUP_WIDTH
        mix = mix + _dot(og, w_ref[lo:lo + B_GROUP_WIDTH, :])
    y_ref[...] = _layer_norm(ALPHA * x_ref[...] + mix, g_ref[...], b_ref[...])


def _ab_out(oa, outs, lses, x, w_out, ln_g, ln_b, tm):
    t = x.shape[0]
    gw = B_GROUP_WIDTH

    def rowblock(width):
        return pl.BlockSpec((tm, width), lambda i: (i, 0))

    def whole(shape):
        return pl.BlockSpec(shape, lambda i: (0, 0))

    return pl.pallas_call(
        _ab_out_body,
        grid=(t // tm,),
        in_specs=[rowblock(A_WIDTH)] + [rowblock(gw)] * 6 + [rowblock(D_MODEL), whole(w_out.shape),
                                                           whole((1, D_MODEL)), whole((1, D_MODEL))],
        out_specs=rowblock(D_MODEL),
        out_shape=jax.ShapeDtypeStruct((t, D_MODEL), F32),
        compiler_params=_cparams("parallel"),
        name="ab_out_ln",
    )(oa, *outs, *lses, x, w_out, ln_g.reshape(1, D_MODEL), ln_b.reshape(1, D_MODEL))


def _mlp_body(x_ref, w1_ref, w2_ref, g_ref, b_ref, y_ref, xb, acc):
    j = pl.program_id(1)

    @pl.when(j == 0)
    def _():
        xb[...] = x_ref[...].astype(BF16)
        acc[...] = jnp.zeros_like(acc)

    h = jnp.square(jnp.maximum(_dot(xb[...], w1_ref[...]), 0.0))
    acc[...] += _dot(h.astype(BF16), w2_ref[...])

    @pl.when(j == pl.num_programs(1) - 1)
    def _():
        y_ref[...] = _layer_norm(ALPHA * x_ref[...] + acc[...], g_ref[...], b_ref[...])


def _mlp(x, w1, w2, ln_g, ln_b, tm, tf):
    t = x.shape[0]
    return pl.pallas_call(
        _mlp_body,
        grid=(t // tm, D_FF // tf),
        in_specs=[pl.BlockSpec((tm, D_MODEL), lambda i, j: (i, 0)),
                  pl.BlockSpec((D_MODEL, tf), lambda i, j: (0, j)),
                  pl.BlockSpec((tf, D_MODEL), lambda i, j: (j, 0)),
                  pl.BlockSpec((1, D_MODEL), lambda i, j: (0, 0)),
                  pl.BlockSpec((1, D_MODEL), lambda i, j: (0, 0))],
        out_specs=pl.BlockSpec((tm, D_MODEL), lambda i, j: (i, 0)),
        out_shape=jax.ShapeDtypeStruct((t, D_MODEL), F32),
        scratch_shapes=[pltpu.VMEM((tm, D_MODEL), BF16), pltpu.VMEM((tm, D_MODEL), F32)],
        compiler_params=_cparams("parallel", "arbitrary"),
        name="mlp_ln",
    )(x, w1, w2, ln_g.reshape(1, D_MODEL), ln_b.reshape(1, D_MODEL))


def _glu_body(x_ref, wa_ref, wg_ref, ba_ref, bg_ref, u_ref):
    xb = x_ref[...].astype(BF16)
    a = _dot(xb, wa_ref[...]) + ba_ref[...]
    gate = _dot(xb, wg_ref[...]) + bg_ref[...]
    u_ref[...] = a * jax.nn.sigmoid(gate)


def _glu_proj(x, w_in, b_in, tm, tn):
    t = x.shape[0]
    nb = D_MODEL // tn
    b2 = b_in.reshape(1, 2 * D_MODEL)
    return pl.pallas_call(
        _glu_body,
        grid=(t // tm, nb),
        in_specs=[pl.BlockSpec((tm, D_MODEL), lambda i, j: (i, 0)),
                  pl.BlockSpec((D_MODEL, tn), lambda i, j: (0, j)),
                  pl.BlockSpec((D_MODEL, tn), lambda i, j: (0, j + nb)),
                  pl.BlockSpec((1, tn), lambda i, j: (0, j)),
                  pl.BlockSpec((1, tn), lambda i, j: (0, j + nb))],
        out_specs=pl.BlockSpec((tm, tn), lambda i, j: (i, j)),
        out_shape=jax.ShapeDtypeStruct((t, D_MODEL), F32),
        compiler_params=_cparams("parallel", "arbitrary"),
        name="conv_glu",
    )(x, w_in, w_in, b2, b2)


def _conv_out_body(u_ref, up_ref, un_ref, x_ref, dw_ref, dwb_ref, ng_ref, nb_ref, w_ref, bo_ref, g_ref, b_ref,
                   y_ref, buf, act, *, ts):
    i = pl.program_id(1)
    halo = CONV_HALO
    pad = CONV_WIDTH // 2
    for s in range(D_MODEL // CONV_LANES):
        cs = slice(s * CONV_LANES, (s + 1) * CONV_LANES)
        buf[s, 0:halo, :] = jnp.where(i > 0, up_ref[0, :, cs], 0.0)
        buf[s, halo:halo + ts, :] = u_ref[0, :, cs]
        buf[s, halo + ts:halo + ts + halo, :] = jnp.where(i < pl.num_programs(1) - 1, un_ref[0, :, cs], 0.0)
    dwb, ng, nb = dwb_ref[...], ng_ref[...], nb_ref[...]

    def tile(rt, carry):
        base = pl.multiple_of(rt * CONV_ROWS, CONV_ROWS)
        cols = []
        for s in range(D_MODEL // CONV_LANES):
            acc = jnp.zeros((CONV_ROWS, CONV_LANES), F32)
            for j in range(CONV_WIDTH):
                rows = pl.ds(base + (halo - pad + j), CONV_ROWS, stride=1)
                acc = acc + dw_ref[j:j + 1, s * CONV_LANES:(s + 1) * CONV_LANES] * buf[s, rows, :]
            cols.append(acc)
        c = _layer_norm(jnp.concatenate(cols, axis=1) + dwb, ng, nb)
        act[pl.ds(base, CONV_ROWS), :] = (c * jax.nn.sigmoid(c)).astype(BF16)
        return carry

    lax.fori_loop(0, ts // CONV_ROWS, tile, 0)
    mix = _dot(act[...], w_ref[...]) + bo_ref[...]
    y_ref[0] = _layer_norm(ALPHA * x_ref[0] + mix, g_ref[...], b_ref[...])


def _conv_out(u, x, dw, dw_b, n_g, n_b, w_out, b_out, ln_g, ln_b, ts):
    bsz, seq, d = x.shape
    hb = ts // CONV_HALO
    last_halo = seq // CONV_HALO - 1

    def vec():
        return pl.BlockSpec((1, d), lambda b, i: (0, 0))

    body = functools.partial(_conv_out_body, ts=ts)
    return pl.pallas_call(
        body,
        grid=(bsz, seq // ts),
        in_specs=[pl.BlockSpec((1, ts, d), lambda b, i: (b, i, 0)),
                  pl.BlockSpec((1, CONV_HALO, d), lambda b, i: (b, jnp.maximum(i * hb - 1, 0), 0)),
                  pl.BlockSpec((1, CONV_HALO, d), lambda b, i: (b, jnp.minimum((i + 1) * hb, last_halo), 0)),
                  pl.BlockSpec((1, ts, d), lambda b, i: (b, i, 0)),
                  pl.BlockSpec((CONV_WIDTH, d), lambda b, i: (0, 0)),
                  vec(), vec(), vec(),
                  pl.BlockSpec((d, d), lambda b, i: (0, 0)),
                  vec(), vec(), vec()],
        out_specs=pl.BlockSpec((1, ts, d), lambda b, i: (b, i, 0)),
        out_shape=jax.ShapeDtypeStruct((bsz, seq, d), F32),
        scratch_shapes=[pltpu.VMEM((d // CONV_LANES, ts + 2 * CONV_HALO, CONV_LANES), F32),
                        pltpu.VMEM((ts, d), BF16)],
        compiler_params=_cparams("parallel", "arbitrary"),
        name="conv_out_ln",
    )(u, u, u, x, dw, dw_b.reshape(1, d), n_g.reshape(1, d), n_b.reshape(1, d), w_out,
      b_out.reshape(1, d), ln_g.reshape(1, d), ln_b.reshape(1, d))


def _tiles(tokens):
    tm = 1024 if tokens % 1024 == 0 else 256
    return tm


def _trunk(x, rel_bias, hgrn_lb, w_in_ab, hgrn_norm, w_out_ab, w_in_c, b_in_c, dw_c, dw_b_c, cnorm_g, cnorm_b,
           w_out_c, b_out_c, ln_mix_g, ln_mix_b, mlp_w1, mlp_w2, ln_ffn_g, ln_ffn_b):
    bsz, seq, d = x.shape
    t = bsz * seq
    tm = _tiles(t)
    x2 = x.reshape(t, d)
    for l in range(DEPTH):
        i = l // 2
        if l % 2 == 0:
            wa = w_in_ab[i][:, :5 * A_WIDTH].astype(BF16)
            wb = w_in_ab[i][:, 5 * A_WIDTH:].astype(BF16)
            ha, hb = _in_proj(x2, wa, wb, IN_PROJ_ROWS)
            oa = _hgrn_mixer(ha.reshape(bsz, seq, 5 * A_WIDTH), hgrn_lb, hgrn_norm[i], l)
            hb3 = hb.reshape(bsz, seq, 3 * B_WIDTH)
            outs, lses = [], []
            for gi, (window, r) in enumerate(B_GROUPS):
                hs = slice(gi * B_HEADS_PER_GROUP, (gi + 1) * B_HEADS_PER_GROUP)
                o, lse = _dilated_group(hb3, rel_bias[:, hs], gi, window, r)
                outs.append(o.reshape(t, B_GROUP_WIDTH))
                lses.append(lse.reshape(t, B_GROUP_WIDTH))
            x2 = _ab_out(oa.reshape(t, A_WIDTH), outs, lses, x2, w_out_ab[i].astype(BF16),
                         ln_mix_g[l], ln_mix_b[l], tm)
        else:
            u = _glu_proj(x2, w_in_c[i].astype(BF16), b_in_c[i], tm, D_MODEL // 2)
            x2 = _conv_out(u.reshape(bsz, seq, d), x2.reshape(bsz, seq, d), dw_c[i], dw_b_c[i], cnorm_g[i],
                           cnorm_b[i], w_out_c[i].astype(BF16), b_out_c[i], ln_mix_g[l], ln_mix_b[l],
                           min(512, seq)).reshape(t, d)
        x2 = _mlp(x2, mlp_w1[l].astype(BF16), mlp_w2[l].astype(BF16), ln_ffn_g[l], ln_ffn_b[l], tm, 1024)
    return x2.reshape(bsz, seq, d)


def kernel(x_prompt, x_sample, rel_bias, hgrn_lb, w_in_ab, hgrn_norm, w_out_ab, w_in_c, b_in_c, dw_c, dw_b_c,
           cnorm_g, cnorm_b, w_out_c, b_out_c, ln_mix_g, ln_mix_b, mlp_w1, mlp_w2, ln_ffn_g, ln_ffn_b):
    params = (rel_bias, hgrn_lb, w_in_ab, hgrn_norm, w_out_ab, w_in_c, b_in_c, dw_c, dw_b_c, cnorm_g, cnorm_b,
              w_out_c, b_out_c, ln_mix_g, ln_mix_b, mlp_w1, mlp_w2, ln_ffn_g, ln_ffn_b)
    return (_trunk(x_prompt, *params), _trunk(x_sample, *params))
```

```python
import functools

import numpy as np
import jax
import jax.numpy as jnp
from jax import lax
from jax.experimental import pallas as pl
from jax.experimental.pallas import tpu as pltpu

F32 = jnp.float32
BF16 = jnp.bfloat16

D_MODEL = 1024
DEPTH = 2
A_HEADS = 4
A_HEAD_DIM = 128
A_WIDTH = A_HEADS * A_HEAD_DIM
B_GROUPS = ((128, 1), (512, 4), (2048, 16))
B_HEADS_PER_GROUP = 4
B_HEAD_DIM = 64
B_GROUP_WIDTH = B_HEADS_PER_GROUP * B_HEAD_DIM
B_WIDTH = B_GROUP_WIDTH * len(B_GROUPS)
CONV_WIDTH = 31
D_FF = 4 * D_MODEL
REL_BUCKETS = 32
REL_MAX_DIST = 1024
ALPHA = (2 * DEPTH) ** 0.25
LN_EPS = 1e-5
RMS_EPS = 1e-6
NEG_INF = -1e30

SUBLANES = 8
IN_PROJ_ROWS = 256
HGRN_CHUNK = 128
ATT_QBLOCK = 128
ATT_LANES = 128
ATT_UNROLL = 4
CONV_BLOCK = 1024
CONV_HALO = 16
CONV_ROWS = 128
CONV_LANES = 128
VMEM_LIMIT = 52 * 1024 * 1024


def _cparams(*sem):
    return pltpu.CompilerParams(dimension_semantics=sem, vmem_limit_bytes=VMEM_LIMIT)


def _dot(a, b):
    return jnp.dot(a, b, preferred_element_type=F32)


def _dot_nt(a, b):
    return lax.dot_general(a, b, (((1,), (1,)), ((), ())), preferred_element_type=F32)


def _dot_tn(a, b):
    return lax.dot_general(a, b, (((0,), (0,)), ((), ())), preferred_element_type=F32)


def _layer_norm(y, g, b):
    mu = jnp.mean(y, -1, keepdims=True)
    yc = y - mu
    var = jnp.mean(jnp.square(yc), -1, keepdims=True)
    return yc * lax.rsqrt(var + LN_EPS) * g + b


HG_PLANES = 7


def _in_proj_body(x_ref, wa_ref, wb_ref, lbp_ref, gn_ref, hg_ref, hb_ref, ha, *, chunk, layer):
    tm = x_ref.shape[0]
    dk = A_HEAD_DIM
    xb = x_ref[...].astype(BF16)
    ha[...] = _dot(xb, wa_ref[...])
    hb_ref[...] = _dot(xb, wb_ref[...])
    row = lax.broadcasted_iota(jnp.int32, (chunk, dk), 0)

    def lower_bound(direction, lanes):
        p = [lbp_ref[direction, l:l + 1, lanes] for l in range(DEPTH + 1)]
        m = functools.reduce(jnp.maximum, p)
        e = [jnp.exp(v - m) for v in p]
        return sum(e[:layer + 1]) / sum(e)

    def gates(z, lb_d):
        log2f = jnp.log2(lb_d + (1.0 - lb_d) * jax.nn.sigmoid(z))
        k = (1.0 - lb_d) * jax.nn.sigmoid(-z)
        return k, log2f

    def plane(p, h):
        return slice(p * A_WIDTH + h * dk, p * A_WIDTH + (h + 1) * dk)

    for h in range(A_HEADS):
        lanes = slice(h * dk, (h + 1) * dk)
        lb_f, lb_b = lower_bound(0, lanes), lower_bound(1, lanes)
        gn = gn_ref[:, lanes]
        for r0 in range(0, tm, chunk):
            rows = slice(r0, r0 + chunk)
            q = ha[rows, plane(0, h)]
            hg_ref[rows, plane(0, h)] = q * jax.nn.sigmoid(q)
            hg_ref[rows, plane(1, h)] = ha[rows, plane(1, h)]
            k, lf = gates(ha[rows, plane(2, h)], lb_f)
            hg_ref[rows, plane(2, h)] = k
            s = 1
            while s < chunk:
                lf = lf + jnp.where(row >= s, pltpu.roll(lf, s, 0), 0.0)
                s *= 2
            hg_ref[rows, plane(4, h)] = lf
            k, lg = gates(ha[rows, plane(3, h)], lb_b)
            hg_ref[rows, plane(3, h)] = k
            s = 1
            while s < chunk:
                lg = lg + jnp.where(row < chunk - s, pltpu.roll(lg, chunk - s, 0), 0.0)
                s *= 2
            hg_ref[rows, plane(5, h)] = lg
            g = ha[rows, plane(4, h)]
            hg_ref[rows, plane(6, h)] = gn * (g * jax.nn.sigmoid(g))


def _in_proj(x, wa, wb, hgrn_lb, g_norm, tm, chunk, layer):
    m, k = x.shape
    na, nb = wa.shape[1], wb.shape[1]
    body = functools.partial(_in_proj_body, chunk=chunk, layer=layer)
    return pl.pallas_call(
        body,
        grid=(m // tm,),
        in_specs=[pl.BlockSpec((tm, k), lambda i: (i, 0)),
                  pl.BlockSpec((k, na), lambda i: (0, 0)),
                  pl.BlockSpec((k, nb), lambda i: (0, 0)),
                  pl.BlockSpec(hgrn_lb.shape, lambda i: (0, 0, 0)),
                  pl.BlockSpec((1, A_WIDTH), lambda i: (0, 0))],
        out_specs=[pl.BlockSpec((tm, HG_PLANES * A_WIDTH), lambda i: (i, 0)),
                   pl.BlockSpec((tm, nb), lambda i: (i, 0))],
        out_shape=[jax.ShapeDtypeStruct((m, HG_PLANES * A_WIDTH), F32), jax.ShapeDtypeStruct((m, nb), F32)],
        scratch_shapes=[pltpu.VMEM((tm, na), F32)],
        compiler_params=_cparams("parallel"),
        name="in_proj",
    )(x, wa, wb, hgrn_lb, g_norm.reshape(1, A_WIDTH))


def _hgrn_body(qs, v_ref, kf, kb, bf, cb, gs, o_ref, of, ob, sf, sb, level, *, seq, chunk):
    C = chunk
    n_chunks = seq // C
    dk = A_HEAD_DIM
    grp = (C // SUBLANES, SUBLANES, dk)

    sf[...] = jnp.zeros_like(sf)
    sb[...] = jnp.zeros_like(sb)
    sub = lax.broadcasted_iota(jnp.int32, grp, 1)
    level[...] = 31 - lax.clz(lax.broadcasted_iota(jnp.int32, (C, C), 0) ^ lax.broadcasted_iota(jnp.int32, (C, C), 1))

    def mid_row(x3, w):
        if w == 1:
            return jnp.where((sub & 1) == 1, x3, pltpu.roll(x3, SUBLANES - 1, 1))
        if w == 2:
            return jnp.where(sub < 4, jnp.broadcast_to(x3[:, 2:3, :], grp), jnp.broadcast_to(x3[:, 6:7, :], grp))
        return jnp.broadcast_to(x3[:, 4:5, :], grp)

    def intra(q, kfv, kbv, bfv, cbv, v):
        lv = level[...]
        scores = jnp.zeros((C, C), F32)
        q3 = q.reshape(grp)
        kf3, kb3, bf3, cb3 = kfv.reshape(grp), kbv.reshape(grp), bfv.reshape(grp), cbv.reshape(grp)
        w = 1
        while w < C:
            if w < SUBLANES:
                second = (sub & w) != 0
                d_f = bf3 - mid_row(bf3, w)
                d_b = cb3 - mid_row(cb3, w)
                qt = (q3 * jnp.exp2(jnp.where(second, d_f, d_b))).reshape(C, dk)
                kt = (jnp.where(second, kb3, kf3) * jnp.exp2(-jnp.where(second, d_b, d_f))).reshape(C, dk)
            else:
                qparts, kparts = [], []
                for r0 in range(0, C, 2 * w):
                    first, second = slice(r0, r0 + w), slice(r0 + w, r0 + 2 * w)
                    bf_r, cb_r = bfv[r0 + w:r0 + w + 1, :], cbv[r0 + w:r0 + w + 1, :]
                    qparts += [q[first] * jnp.exp2(cbv[first] - cb_r), q[second] * jnp.exp2(bfv[second] - bf_r)]
                    kparts += [kfv[first] * jnp.exp2(bf_r - bfv[first]), kbv[second] * jnp.exp2(cb_r - cbv[second])]
                qt, kt = jnp.concatenate(qparts, 0), jnp.concatenate(kparts, 0)
            p = _dot_nt(qt.astype(BF16), kt.astype(BF16))
            scores = jnp.where(lv == w.bit_length() - 1, p, scores)
            w *= 2
        same = jnp.sum(q * (kfv + kbv), axis=-1, keepdims=True) * v
        return same + _dot(scores.astype(BF16), v.astype(BF16))

    def step(n, carry):
        rows = pl.ds(pl.multiple_of(n * C, C), C)
        q, kfv, kbv, bfv, cbv, v = qs[rows, :], kf[rows, :], kb[rows, :], bf[rows, :], cb[rows, :], v_ref[rows, :]
        o = intra(q, kfv, kbv, bfv, cbv, v)
        st = sf[...]
        o = o + _dot_nt((q * jnp.exp2(bfv)).astype(BF16), st.astype(BF16))
        of[rows, :] = o
        b_last = bfv[C - 1:C, :]
        kt = (kfv * jnp.exp2(b_last - bfv)).astype(BF16)
        sf[...] = jnp.exp2(b_last) * st + _dot_tn(v.astype(BF16), kt)
        rows = pl.ds(pl.multiple_of((n_chunks - 1 - n) * C, C), C)
        q, kbv, cbv, v = qs[rows, :], kb[rows, :], cb[rows, :], v_ref[rows, :]
        st = sb[...]
        ob[rows, :] = _dot_nt((q * jnp.exp2(cbv)).astype(BF16), st.astype(BF16))
        c_first = cbv[0:1, :]
        kt = (kbv * jnp.exp2(c_first - cbv)).astype(BF16)
        sb[...] = jnp.exp2(c_first) * st + _dot_tn(v.astype(BF16), kt)
        return carry

    lax.fori_loop(0, n_chunks, step, 0, unroll=2 if n_chunks % 2 == 0 else 1)

    def finish(n, carry):
        rows = pl.ds(pl.multiple_of(n * C, C), C)
        o = of[rows, :] + ob[rows, :]
        o = o * lax.rsqrt(jnp.mean(o * o, -1, keepdims=True) + RMS_EPS)
        o_ref[rows, :] = (o * gs[rows, :]).astype(o_ref.dtype)
        return carry

    lax.fori_loop(0, n_chunks, finish, 0)


def _hgrn_mixer(hg, chunk):
    bsz, seq, _ = hg.shape
    dk = A_HEAD_DIM

    def col(part):
        return pl.BlockSpec((None, seq, dk), lambda b, h, part=part: (b, 0, part * A_HEADS + h))

    body = functools.partial(_hgrn_body, seq=seq, chunk=chunk)
    return pl.pallas_call(
        body,
        grid=(bsz, A_HEADS),
        in_specs=[col(p) for p in range(HG_PLANES)],
        out_specs=pl.BlockSpec((None, seq, dk), lambda b, h: (b, 0, h)),
        out_shape=jax.ShapeDtypeStruct((bsz, seq, A_WIDTH), BF16),
        scratch_shapes=[pltpu.VMEM((seq, dk), F32)] * 2 + [pltpu.VMEM((dk, dk), F32)] * 2
        + [pltpu.VMEM((chunk, chunk), jnp.int32)],
        compiler_params=_cparams("parallel", "arbitrary"),
        name="hgrn2",
    )(*([hg] * HG_PLANES))


def _t5_buckets(rel):
    half = REL_BUCKETS // 2
    max_exact = half // 2
    n = np.abs(rel)
    large = max_exact + (np.log(np.maximum(n, 1) / max_exact) / np.log(REL_MAX_DIST / max_exact)
                         * (half - max_exact)).astype(np.int32)
    large = np.minimum(large, half - 1)
    return (np.where(rel > 0, half, 0) + np.where(n < max_exact, n, large)).astype(np.int32)


def _att_geometry(sub_len, half):
    lq = min(ATT_QBLOCK, sub_len)
    win = min(lq + 2 * half, sub_len)
    return lq, win, sub_len // lq


def _att_bias(rel_bias_g, sub_len, dilation, half):
    lq, win, _ = _att_geometry(sub_len, half)
    p = win + lq - 1
    tables = []
    for delta in (0, -half, lq - win):
        rel = np.arange(p + 1) - (lq - 1) + delta
        valid = (np.abs(rel) <= half) & (np.arange(p + 1) < p)
        u = jnp.where(valid[:, None], rel_bias_g[_t5_buckets(rel * dilation)], NEG_INF).T
        flat = jnp.tile(u, (1, lq))[:, :lq * p]
        tables.append(flat.reshape(-1, lq, p)[:, :, lq - 1:lq - 1 + win])
    return jnp.stack(tables).astype(F32)


def _att_body(q_ref, k_ref, v_ref, bias_ref, o_ref, lse_ref, qd, kd, vd, oacc, *, seq, r, lq, win, half):
    sub_len = seq // r
    nq = sub_len // lq
    n_tasks = r * nq
    unroll = min(ATT_UNROLL, n_tasks)
    lanes = q_ref.shape[-1]
    heads = lanes // B_HEAD_DIM
    lane = lax.broadcasted_iota(jnp.int32, (lq, lanes), 1)
    head_lanes = [(lane >= h * B_HEAD_DIM) & (lane < (h + 1) * B_HEAD_DIM) for h in range(heads)]

    def split(c, carry):
        members = pl.ds(c, sub_len, stride=r)
        dst = pl.ds(pl.multiple_of(c * sub_len, sub_len), sub_len)
        qd[dst, :] = (q_ref[members, :] * (B_HEAD_DIM ** -0.5)).astype(BF16)
        kd[dst, :] = k_ref[members, :].astype(BF16)
        vd[dst, :] = v_ref[members, :].astype(BF16)
        return carry

    lax.fori_loop(0, r, split, 0)

    def task(tk):
        c = tk // nq
        iq = tk % nq
        q0 = iq * lq
        ws = jnp.clip(q0 - half, 0, sub_len - win)
        case = jnp.where(iq == 0, 0, jnp.where(iq == nq - 1, 2, 1))
        base = c * sub_len
        q = qd[pl.ds(pl.multiple_of(base + q0, lq), lq), :]
        kwin = kd[pl.ds(pl.multiple_of(base + ws, half), win), :]
        vwin = vd[pl.ds(pl.multiple_of(base + ws, half), win), :]
        qm = jnp.concatenate([jnp.where(hl, q, jnp.zeros_like(q)) for hl in head_lanes], axis=0)
        bias = jnp.concatenate([bias_ref[case, h] for h in range(heads)], axis=0)
        s = _dot_nt(qm, kwin) + bias
        m = jnp.max(s, -1, keepdims=True)
        p = jnp.exp(s - m)
        den = jnp.sum(p, -1, keepdims=True)
        o2 = _dot(p.astype(BF16), vwin) / den
        lse2 = m + jnp.log(den)
        out = o2[0:lq]
        lse = jnp.broadcast_to(lse2[0:lq], (lq, lanes))
        for h in range(1, heads):
            out = jnp.where(head_lanes[h], o2[h * lq:(h + 1) * lq], out)
            lse = jnp.where(head_lanes[h], lse2[h * lq:(h + 1) * lq], lse)
        dst = pl.ds(c + q0 * r, lq, stride=r)
        oacc[dst, :] = out
        lse_ref[dst, :] = lse

    def group(i, carry):
        for u in range(unroll):
            task(i * unroll + u)
        return carry

    lax.fori_loop(0, n_tasks // unroll, group, 0)
    o_ref[...] = oacc[...].astype(o_ref.dtype)


def _dilated_group(hb, rel_bias_g, gi, window, dilation):
    bsz, seq, _ = hb.shape
    r = dilation
    sub_len = seq // r
    half = window // (2 * r)
    lq, win, nq = _att_geometry(sub_len, half)
    lanes = ATT_LANES
    pairs = B_GROUP_WIDTH // lanes
    heads = lanes // B_HEAD_DIM
    bias = _att_bias(rel_bias_g, sub_len, r, half)

    def part(p):
        return pl.BlockSpec((None, seq, lanes), lambda b, hp, p=p: (b, 0, (p * len(B_GROUPS) + gi) * pairs + hp))

    def out_spec():
        return pl.BlockSpec((None, seq, lanes), lambda b, hp: (b, 0, hp))

    body = functools.partial(_att_body, seq=seq, r=r, lq=lq, win=win, half=half)
    return pl.pallas_call(
        body,
        grid=(bsz, pairs),
        in_specs=[part(0), part(1), part(2),
                  pl.BlockSpec((3, heads, lq, win), lambda b, hp: (0, hp, 0, 0))],
        out_specs=[out_spec(), out_spec()],
        out_shape=[jax.ShapeDtypeStruct((bsz, seq, B_GROUP_WIDTH), BF16),
                   jax.ShapeDtypeStruct((bsz, seq, B_GROUP_WIDTH), F32)],
        scratch_shapes=[pltpu.VMEM((seq, lanes), BF16)] * 3 + [pltpu.VMEM((seq, lanes), F32)],
        compiler_params=_cparams("parallel", "arbitrary"),
        name=f"dilated_attn_r{r}",
    )(hb, hb, hb, bias)


def _ab_out_body(oa_ref, o1_ref, o2_ref, o3_ref, l1_ref, l2_ref, l3_ref, x_ref, w_ref, g_ref, b_ref, y_ref):
    l1, l2, l3 = l1_ref[...], l2_ref[...], l3_ref[...]
    m = jnp.maximum(jnp.maximum(l1, l2), l3)
    e1, e2, e3 = jnp.exp(l1 - m), jnp.exp(l2 - m), jnp.exp(l3 - m)
    tot = e1 + e2 + e3
    mix = _dot(oa_ref[...], w_ref[0:A_WIDTH, :])
    for gi, (o_ref, e) in enumerate(((o1_ref, e1), (o2_ref, e2), (o3_ref, e3))):
        og = (o_ref[...].astype(F32) * (e / tot)).astype(BF16)
        lo = A_WIDTH + gi * B_GROUP_WIDTH
        mix = mix + _dot(og, w_ref[lo:lo + B_GROUP_WIDTH, :])
    y_ref[...] = _layer_norm(ALPHA * x_ref[...] + mix, g_ref[...], b_ref[...])


def _ab_out(oa, outs, lses, x, w_out, ln_g, ln_b, tm):
    t = x.shape[0]
    gw = B_GROUP_WIDTH

    def rowblock(width):
        return pl.BlockSpec((tm, width), lambda i: (i, 0))

    def whole(shape):
        return pl.BlockSpec(shape, lambda i: (0, 0))

    return pl.pallas_call(
        _ab_out_body,
        grid=(t // tm,),
        in_specs=[rowblock(A_WIDTH)] + [rowblock(gw)] * 6 + [rowblock(D_MODEL), whole(w_out.shape),
                                                           whole((1, D_MODEL)), whole((1, D_MODEL))],
        out_specs=rowblock(D_MODEL),
        out_shape=jax.ShapeDtypeStruct((t, D_MODEL), F32),
        compiler_params=_cparams("parallel"),
        name="ab_out_ln",
    )(oa, *outs, *lses, x, w_out, ln_g.reshape(1, D_MODEL), ln_b.reshape(1, D_MODEL))


def _mlp_body(x_ref, w1_ref, w2_ref, g_ref, b_ref, y_ref, xb, acc):
    j = pl.program_id(1)

    @pl.when(j == 0)
    def _():
        xb[...] = x_ref[...].astype(BF16)
        acc[...] = jnp.zeros_like(acc)

    h = jnp.square(jnp.maximum(_dot(xb[...], w1_ref[...]), 0.0))
    acc[...] += _dot(h.astype(BF16), w2_ref[...])

    @pl.when(j == pl.num_programs(1) - 1)
    def _():
        y_ref[...] = _layer_norm(ALPHA * x_ref[...] + acc[...], g_ref[...], b_ref[...])


def _mlp(x, w1, w2, ln_g, ln_b, tm, tf):
    t = x.shape[0]
    return pl.pallas_call(
        _mlp_body,
        grid=(t // tm, D_FF // tf),
        in_specs=[pl.BlockSpec((tm, D_MODEL), lambda i, j: (i, 0)),
                  pl.BlockSpec((D_MODEL, tf), lambda i, j: (0, j)),
                  pl.BlockSpec((tf, D_MODEL), lambda i, j: (j, 0)),
                  pl.BlockSpec((1, D_MODEL), lambda i, j: (0, 0)),
                  pl.BlockSpec((1, D_MODEL), lambda i, j: (0, 0))],
        out_specs=pl.BlockSpec((tm, D_MODEL), lambda i, j: (i, 0)),
        out_shape=jax.ShapeDtypeStruct((t, D_MODEL), F32),
        scratch_shapes=[pltpu.VMEM((tm, D_MODEL), BF16), pltpu.VMEM((tm, D_MODEL), F32)],
        compiler_params=_cparams("parallel", "arbitrary"),
        name="mlp_ln",
    )(x, w1, w2, ln_g.reshape(1, D_MODEL), ln_b.reshape(1, D_MODEL))


def _glu_body(x_ref, wa_ref, wg_ref, ba_ref, bg_ref, u_ref):
    xb = x_ref[...].astype(BF16)
    a = _dot(xb, wa_ref[...]) + ba_ref[...]
    gate = _dot(xb, wg_ref[...]) + bg_ref[...]
    u_ref[...] = a * jax.nn.sigmoid(gate)


def _glu_proj(x, w_in, b_in, tm, tn):
    t = x.shape[0]
    nb = D_MODEL // tn
    b2 = b_in.reshape(1, 2 * D_MODEL)
    return pl.pallas_call(
        _glu_body,
        grid=(t // tm, nb),
        in_specs=[pl.BlockSpec((tm, D_MODEL), lambda i, j: (i, 0)),
                  pl.BlockSpec((D_MODEL, tn), lambda i, j: (0, j)),
                  pl.BlockSpec((D_MODEL, tn), lambda i, j: (0, j + nb)),
                  pl.BlockSpec((1, tn), lambda i, j: (0, j)),
                  pl.BlockSpec((1, tn), lambda i, j: (0, j + nb))],
        out_specs=pl.BlockSpec((tm, tn), lambda i, j: (i, j)),
        out_shape=jax.ShapeDtypeStruct((t, D_MODEL), F32),
        compiler_params=_cparams("parallel", "arbitrary"),
        name="conv_glu",
    )(x, w_in, w_in, b2, b2)


def _conv_out_body(u_ref, up_ref, un_ref, x_ref, dw_ref, dwb_ref, ng_ref, nb_ref, w_ref, bo_ref, g_ref, b_ref,
                   y_ref, buf, act, *, ts):
    i = pl.program_id(1)
    halo = CONV_HALO
    pad = CONV_WIDTH // 2
    for s in range(D_MODEL // CONV_LANES):
        cs = slice(s * CONV_LANES, (s + 1) * CONV_LANES)
        buf[s, 0:halo, :] = jnp.where(i > 0, up_ref[0, :, cs], 0.0)
        buf[s, halo:halo + ts, :] = u_ref[0, :, cs]
        buf[s, halo + ts:halo + ts + halo, :] = jnp.where(i < pl.num_programs(1) - 1, un_ref[0, :, cs], 0.0)
    dwb, ng, nb = dwb_ref[...], ng_ref[...], nb_ref[...]
    n_tiles = ts // CONV_ROWS

    def conv_tile(t):
        base = pl.multiple_of(t * CONV_ROWS, CONV_ROWS)
        n_groups = CONV_ROWS // SUBLANES
        cols = []
        for s in range(D_MODEL // CONV_LANES):
            lanes = slice(s * CONV_LANES, (s + 1) * CONV_LANES)
            accs = [jnp.zeros((SUBLANES, CONV_LANES), F32) for _ in range(n_groups)]
            for o in range(halo - pad, CONV_ROWS + halo + pad - SUBLANES + 1):
                data = buf[s, pl.ds(base + o, SUBLANES, stride=1), :]
                for g in range(n_groups):
                    j = o - (halo - pad) - SUBLANES * g
                    if 0 <= j < CONV_WIDTH:
                        accs[g] = accs[g] + dw_ref[j:j + 1, lanes] * data
            cols.append(jnp.concatenate(accs, axis=0))
        c = _layer_norm(jnp.concatenate(cols, axis=1) + dwb, ng, nb)
        act[t % 2] = (c * jax.nn.sigmoid(c)).astype(BF16)

    def out_tile(t):
        rows = pl.ds(pl.multiple_of(t * CONV_ROWS, CONV_ROWS), CONV_ROWS)
        mix = _dot(act[t % 2], w_ref[...]) + bo_ref[...]
        y_ref[0, rows, :] = _layer_norm(ALPHA * x_ref[0, rows, :] + mix, g_ref[...], b_ref[...])

    conv_tile(0)

    def pipelined(t, carry):
        conv_tile(t)
        out_tile(t - 1)
        return carry

    lax.fori_loop(1, n_tiles, pipelined, 0)
    out_tile(n_tiles - 1)


def _conv_out(u, x, dw, dw_b, n_g, n_b, w_out, b_out, ln_g, ln_b, ts):
    bsz, seq, d = x.shape
    hb = ts // CONV_HALO
    last_halo = seq // CONV_HALO - 1

    def vec():
        return pl.BlockSpec((1, d), lambda b, i: (0, 0))

    body = functools.partial(_conv_out_body, ts=ts)
    return pl.pallas_call(
        body,
        grid=(bsz, seq // ts),
        in_specs=[pl.BlockSpec((1, ts, d), lambda b, i: (b, i, 0)),
                  pl.BlockSpec((1, CONV_HALO, d), lambda b, i: (b, jnp.maximum(i * hb - 1, 0), 0)),
                  pl.BlockSpec((1, CONV_HALO, d), lambda b, i: (b, jnp.minimum((i + 1) * hb, last_halo), 0)),
                  pl.BlockSpec((1, ts, d), lambda b, i: (b, i, 0)),
                  pl.BlockSpec((CONV_WIDTH, d), lambda b, i: (0, 0)),
                  vec(), vec(), vec(),
                  pl.BlockSpec((d, d), lambda b, i: (0, 0)),
                  vec(), vec(), vec()],
        out_specs=pl.BlockSpec((1, ts, d), lambda b, i: (b, i, 0)),
        out_shape=jax.ShapeDtypeStruct((bsz, seq, d), F32),
        scratch_shapes=[pltpu.VMEM((d // CONV_LANES, ts + 2 * CONV_HALO, CONV_LANES), F32),
                        pltpu.VMEM((2, CONV_ROWS, d), BF16)],
        compiler_params=_cparams("parallel", "arbitrary"),
        name="conv_out_ln",
    )(u, u, u, x, dw, dw_b.reshape(1, d), n_g.reshape(1, d), n_b.reshape(1, d), w_out,
      b_out.reshape(1, d), ln_g.reshape(1, d), ln_b.reshape(1, d))


def _tiles(tokens):
    tm = 1024 if tokens % 1024 == 0 else 256
    return tm


def _trunk(x, rel_bias, hgrn_lb, w_in_ab, hgrn_norm, w_out_ab, w_in_c, b_in_c, dw_c, dw_b_c, cnorm_g, cnorm_b,
           w_out_c, b_out_c, ln_mix_g, ln_mix_b, mlp_w1, mlp_w2, ln_ffn_g, ln_ffn_b):
    bsz, seq, d = x.shape
    t = bsz * seq
    tm = _tiles(t)
    x2 = x.reshape(t, d)
    for l in range(DEPTH):
        i = l // 2
        if l % 2 == 0:
            wa = w_in_ab[i][:, :5 * A_WIDTH].astype(BF16)
            wb = w_in_ab[i][:, 5 * A_WIDTH:].astype(BF16)
            chunk = min(HGRN_CHUNK, seq)
            hg, hb = _in_proj(x2, wa, wb, hgrn_lb, hgrn_norm[i], min(IN_PROJ_ROWS, seq), chunk, l)
            oa = _hgrn_mixer(hg.reshape(bsz, seq, HG_PLANES * A_WIDTH), chunk)
            hb3 = hb.reshape(bsz, seq, 3 * B_WIDTH)
            outs, lses = [], []
            for gi, (window, r) in enumerate(B_GROUPS):
                hs = slice(gi * B_HEADS_PER_GROUP, (gi + 1) * B_HEADS_PER_GROUP)
                o, lse = _dilated_group(hb3, rel_bias[:, hs], gi, window, r)
                outs.append(o.reshape(t, B_GROUP_WIDTH))
                lses.append(lse.reshape(t, B_GROUP_WIDTH))
            x2 = _ab_out(oa.reshape(t, A_WIDTH), outs, lses, x2, w_out_ab[i].astype(BF16),
                         ln_mix_g[l], ln_mix_b[l], tm)
        else:
            u = _glu_proj(x2, w_in_c[i].astype(BF16), b_in_c[i], tm, D_MODEL // 2)
            x2 = _conv_out(u.reshape(bsz, seq, d), x2.reshape(bsz, seq, d), dw_c[i], dw_b_c[i], cnorm_g[i],
                           cnorm_b[i], w_out_c[i].astype(BF16), b_out_c[i], ln_mix_g[l], ln_mix_b[l],
                           min(CONV_BLOCK, seq)).reshape(t, d)
        x2 = _mlp(x2, mlp_w1[l].astype(BF16), mlp_w2[l].astype(BF16), ln_ffn_g[l], ln_ffn_b[l], tm, 1024)
    return x2.reshape(bsz, seq, d)


def kernel(x_prompt, x_sample, rel_bias, hgrn_lb, w_in_ab, hgrn_norm, w_out_ab, w_in_c, b_in_c, dw_c, dw_b_c,
           cnorm_g, cnorm_b, w_out_c, b_out_c, ln_mix_g, ln_mix_b, mlp_w1, mlp_w2, ln_ffn_g, ln_ffn_b):
    params = (rel_bias, hgrn_lb, w_in_ab, hgrn_norm, w_out_ab, w_in_c, b_in_c, dw_c, dw_b_c, cnorm_g, cnorm_b,
              w_out_c, b_out_c, ln_mix_g, ln_mix_b, mlp_w1, mlp_w2, ln_ffn_g, ln_ffn_b)
    return (_trunk(x_prompt, *params), _trunk(x_sample, *params))
```

```python
import functools

import numpy as np
import jax
import jax.numpy as jnp
from jax import lax
from jax.experimental import pallas as pl
from jax.experimental.pallas import tpu as pltpu

F32 = jnp.float32
BF16 = jnp.bfloat16

D_MODEL = 1024
DEPTH = 2
A_HEADS = 4
A_HEAD_DIM = 128
A_WIDTH = A_HEADS * A_HEAD_DIM
B_GROUPS = ((128, 1), (512, 4), (2048, 16))
B_HEADS_PER_GROUP = 4
B_HEAD_DIM = 64
B_GROUP_WIDTH = B_HEADS_PER_GROUP * B_HEAD_DIM
B_WIDTH = B_GROUP_WIDTH * len(B_GROUPS)
CONV_WIDTH = 31
D_FF = 4 * D_MODEL
REL_BUCKETS = 32
REL_MAX_DIST = 1024
ALPHA = (2 * DEPTH) ** 0.25
LN_EPS = 1e-5
RMS_EPS = 1e-6
NEG_INF = -1e30

SUBLANES = 8
IN_PROJ_ROWS = 256
HGRN_CHUNK = 128
ATT_QBLOCK = 128
ATT_LANES = 128
ATT_UNROLL = 4
GLU_ROWS = 512
GLU_COLS = 512
CONV_BLOCK = 1024
CONV_HALO = 16
CONV_ROWS = 128
CONV_LANES = 128
VMEM_LIMIT = 52 * 1024 * 1024


def _cparams(*sem):
    return pltpu.CompilerParams(dimension_semantics=sem, vmem_limit_bytes=VMEM_LIMIT)


def _dot(a, b):
    return jnp.dot(a, b, preferred_element_type=F32)


def _dot_nt(a, b):
    return lax.dot_general(a, b, (((1,), (1,)), ((), ())), preferred_element_type=F32)


def _dot_tn(a, b):
    return lax.dot_general(a, b, (((0,), (0,)), ((), ())), preferred_element_type=F32)


def _layer_norm(y, g, b):
    mu = jnp.mean(y, -1, keepdims=True)
    yc = y - mu
    var = jnp.mean(jnp.square(yc), -1, keepdims=True)
    return yc * lax.rsqrt(var + LN_EPS) * g + b


HG_PLANES = 7


def _in_proj_body(x_ref, wa_ref, wb_ref, lbp_ref, gn_ref, hg_ref, hb_ref, ha, *, chunk, layer):
    tm = x_ref.shape[0]
    dk = A_HEAD_DIM
    xb = x_ref[...].astype(BF16)
    row = lax.broadcasted_iota(jnp.int32, (chunk, dk), 0)

    def lower_bound(direction, lanes):
        p = [lbp_ref[direction, l:l + 1, lanes] for l in range(DEPTH + 1)]
        m = functools.reduce(jnp.maximum, p)
        e = [jnp.exp(v - m) for v in p]
        return sum(e[:layer + 1]) / sum(e)

    def gates(z, lb_d):
        log2f = jnp.log2(lb_d + (1.0 - lb_d) * jax.nn.sigmoid(z))
        k = (1.0 - lb_d) * jax.nn.sigmoid(-z)
        return k, log2f

    def plane(p, h):
        return slice(p * A_WIDTH + h * dk, p * A_WIDTH + (h + 1) * dk)

    head_cols = wa_ref.shape[1] // A_HEADS
    for hp in range(A_HEADS // 2):
        ha[hp] = _dot(xb, wa_ref[:, hp * 2 * head_cols:(hp + 1) * 2 * head_cols])
    hb_ref[...] = _dot(xb, wb_ref[...])
    for h in range(A_HEADS):
        lanes = slice(h * dk, (h + 1) * dk)
        lb_f, lb_b = lower_bound(0, lanes), lower_bound(1, lanes)
        gn = gn_ref[:, lanes]

        def part(p, rows):
            c0 = (h % 2) * head_cols + p * dk
            return ha[h // 2, rows, c0:c0 + dk]

        for r0 in range(0, tm, chunk):
            rows = slice(r0, r0 + chunk)
            q = part(0, rows)
            hg_ref[rows, plane(0, h)] = q * jax.nn.sigmoid(q)
            hg_ref[rows, plane(1, h)] = part(1, rows)
            k, lf = gates(part(2, rows), lb_f)
            hg_ref[rows, plane(2, h)] = k
            s = 1
            while s < chunk:
                lf = lf + jnp.where(row >= s, pltpu.roll(lf, s, 0), 0.0)
                s *= 2
            hg_ref[rows, plane(4, h)] = lf
            k, lg = gates(part(3, rows), lb_b)
            hg_ref[rows, plane(3, h)] = k
            s = 1
            while s < chunk:
                lg = lg + jnp.where(row < chunk - s, pltpu.roll(lg, chunk - s, 0), 0.0)
                s *= 2
            hg_ref[rows, plane(5, h)] = lg
            g = part(4, rows)
            hg_ref[rows, plane(6, h)] = gn * (g * jax.nn.sigmoid(g))


def _in_proj(x, wa, wb, hgrn_lb, g_norm, tm, chunk, layer):
    m, k = x.shape
    na, nb = wa.shape[1], wb.shape[1]
    body = functools.partial(_in_proj_body, chunk=chunk, layer=layer)
    return pl.pallas_call(
        body,
        grid=(m // tm,),
        in_specs=[pl.BlockSpec((tm, k), lambda i: (i, 0)),
                  pl.BlockSpec((k, na), lambda i: (0, 0)),
                  pl.BlockSpec((k, nb), lambda i: (0, 0)),
                  pl.BlockSpec(hgrn_lb.shape, lambda i: (0, 0, 0)),
                  pl.BlockSpec((1, A_WIDTH), lambda i: (0, 0))],
        out_specs=[pl.BlockSpec((tm, HG_PLANES * A_WIDTH), lambda i: (i, 0)),
                   pl.BlockSpec((tm, nb), lambda i: (i, 0))],
        out_shape=[jax.ShapeDtypeStruct((m, HG_PLANES * A_WIDTH), F32), jax.ShapeDtypeStruct((m, nb), F32)],
        scratch_shapes=[pltpu.VMEM((A_HEADS // 2, tm, 2 * na // A_HEADS), F32)],
        compiler_params=_cparams("parallel"),
        name="in_proj",
    )(x, wa, wb, hgrn_lb, g_norm.reshape(1, A_WIDTH))


def _hgrn_body(qs, v_ref, kf, kb, bf, cb, gs, o_ref, of, ob, sf, sb, level, *, seq, chunk):
    C = chunk
    n_chunks = seq // C
    dk = A_HEAD_DIM
    grp = (C // SUBLANES, SUBLANES, dk)

    sf[...] = jnp.zeros_like(sf)
    sb[...] = jnp.zeros_like(sb)
    sub = lax.broadcasted_iota(jnp.int32, grp, 1)
    level[...] = 31 - lax.clz(lax.broadcasted_iota(jnp.int32, (C, C), 0) ^ lax.broadcasted_iota(jnp.int32, (C, C), 1))

    def mid_row(x3, w):
        if w == 1:
            return jnp.where((sub & 1) == 1, x3, pltpu.roll(x3, SUBLANES - 1, 1))
        if w == 2:
            return jnp.where(sub < 4, jnp.broadcast_to(x3[:, 2:3, :], grp), jnp.broadcast_to(x3[:, 6:7, :], grp))
        return jnp.broadcast_to(x3[:, 4:5, :], grp)

    def intra(q, kfv, kbv, bfv, cbv, v):
        lv = level[...]
        scores = jnp.zeros((C, C), F32)
        q3 = q.reshape(grp)
        kf3, kb3, bf3, cb3 = kfv.reshape(grp), kbv.reshape(grp), bfv.reshape(grp), cbv.reshape(grp)
        w = 1
        while w < C:
            if w < SUBLANES:
                second = (sub & w) != 0
                d_f = bf3 - mid_row(bf3, w)
                d_b = cb3 - mid_row(cb3, w)
                qt = (q3 * jnp.exp2(jnp.where(second, d_f, d_b))).reshape(C, dk)
                kt = (jnp.where(second, kb3, kf3) * jnp.exp2(-jnp.where(second, d_b, d_f))).reshape(C, dk)
            else:
                qparts, kparts = [], []
                for r0 in range(0, C, 2 * w):
                    first, second = slice(r0, r0 + w), slice(r0 + w, r0 + 2 * w)
                    bf_r, cb_r = bfv[r0 + w:r0 + w + 1, :], cbv[r0 + w:r0 + w + 1, :]
                    qparts += [q[first] * jnp.exp2(cbv[first] - cb_r), q[second] * jnp.exp2(bfv[second] - bf_r)]
                    kparts += [kfv[first] * jnp.exp2(bf_r - bfv[first]), kbv[second] * jnp.exp2(cb_r - cbv[second])]
                qt, kt = jnp.concatenate(qparts, 0), jnp.concatenate(kparts, 0)
            p = _dot_nt(qt.astype(BF16), kt.astype(BF16))
            scores = jnp.where(lv == w.bit_length() - 1, p, scores)
            w *= 2
        same = jnp.sum(q * (kfv + kbv), axis=-1, keepdims=True) * v
        return same + _dot(scores.astype(BF16), v.astype(BF16))

    def step(n, carry):
        rows = pl.ds(pl.multiple_of(n * C, C), C)
        q, kfv, kbv, bfv, cbv, v = qs[rows, :], kf[rows, :], kb[rows, :], bf[rows, :], cb[rows, :], v_ref[rows, :]
        o = intra(q, kfv, kbv, bfv, cbv, v)
        st = sf[...]
        o = o + _dot_nt((q * jnp.exp2(bfv)).astype(BF16), st.astype(BF16))
        of[rows, :] = o
        b_last = bfv[C - 1:C, :]
        kt = (kfv * jnp.exp2(b_last - bfv)).astype(BF16)
        sf[...] = jnp.exp2(b_last) * st + _dot_tn(v.astype(BF16), kt)
        rows = pl.ds(pl.multiple_of((n_chunks - 1 - n) * C, C), C)
        q, kbv, cbv, v = qs[rows, :], kb[rows, :], cb[rows, :], v_ref[rows, :]
        st = sb[...]
        ob[rows, :] = _dot_nt((q * jnp.exp2(cbv)).astype(BF16), st.astype(BF16))
        c_first = cbv[0:1, :]
        kt = (kbv * jnp.exp2(c_first - cbv)).astype(BF16)
        sb[...] = jnp.exp2(c_first) * st + _dot_tn(v.astype(BF16), kt)
        return carry

    lax.fori_loop(0, n_chunks, step, 0, unroll=2 if n_chunks % 2 == 0 else 1)

    def finish(n, carry):
        rows = pl.ds(pl.multiple_of(n * C, C), C)
        o = of[rows, :] + ob[rows, :]
        o = o * lax.rsqrt(jnp.mean(o * o, -1, keepdims=True) + RMS_EPS)
        o_ref[rows, :] = (o * gs[rows, :]).astype(o_ref.dtype)
        return carry

    lax.fori_loop(0, n_chunks, finish, 0)


def _hgrn_mixer(hg, chunk):
    bsz, seq, _ = hg.shape
    dk = A_HEAD_DIM

    def col(part):
        return pl.BlockSpec((None, seq, dk), lambda b, h, part=part: (b, 0, part * A_HEADS + h))

    body = functools.partial(_hgrn_body, seq=seq, chunk=chunk)
    return pl.pallas_call(
        body,
        grid=(bsz, A_HEADS),
        in_specs=[col(p) for p in range(HG_PLANES)],
        out_specs=pl.BlockSpec((None, seq, dk), lambda b, h: (b, 0, h)),
        out_shape=jax.ShapeDtypeStruct((bsz, seq, A_WIDTH), BF16),
        scratch_shapes=[pltpu.VMEM((seq, dk), F32)] * 2 + [pltpu.VMEM((dk, dk), F32)] * 2
        + [pltpu.VMEM((chunk, chunk), jnp.int32)],
        compiler_params=_cparams("parallel", "arbitrary"),
        name="hgrn2",
    )(*([hg] * HG_PLANES))


def _t5_buckets(rel):
    half = REL_BUCKETS // 2
    max_exact = half // 2
    n = np.abs(rel)
    large = max_exact + (np.log(np.maximum(n, 1) / max_exact) / np.log(REL_MAX_DIST / max_exact)
                         * (half - max_exact)).astype(np.int32)
    large = np.minimum(large, half - 1)
    return (np.where(rel > 0, half, 0) + np.where(n < max_exact, n, large)).astype(np.int32)


def _att_geometry(sub_len, half):
    lq = min(ATT_QBLOCK, sub_len)
    win = min(lq + 2 * half, sub_len)
    return lq, win, sub_len // lq


def _att_bias(rel_bias_g, sub_len, dilation, half):
    lq, win, _ = _att_geometry(sub_len, half)
    p = win + lq - 1
    tables = []
    for delta in (0, -half, lq - win):
        rel = np.arange(p + 1) - (lq - 1) + delta
        valid = (np.abs(rel) <= half) & (np.arange(p + 1) < p)
        u = jnp.where(valid[:, None], rel_bias_g[_t5_buckets(rel * dilation)], NEG_INF).T
        flat = jnp.tile(u, (1, lq))[:, :lq * p]
        tables.append(flat.reshape(-1, lq, p)[:, :, lq - 1:lq - 1 + win])
    return jnp.stack(tables).astype(F32)


def _att_body(q_ref, k_ref, v_ref, bias_ref, o_ref, lse_ref, qd, kd, vd, oacc, *, seq, r, lq, win, half):
    sub_len = seq // r
    nq = sub_len // lq
    n_tasks = r * nq
    unroll = min(ATT_UNROLL, n_tasks)
    lanes = q_ref.shape[-1]
    heads = lanes // B_HEAD_DIM
    lane = lax.broadcasted_iota(jnp.int32, (lq, lanes), 1)
    head_lanes = [(lane >= h * B_HEAD_DIM) & (lane < (h + 1) * B_HEAD_DIM) for h in range(heads)]

    def split(c, carry):
        members = pl.ds(c, sub_len, stride=r)
        dst = pl.ds(pl.multiple_of(c * sub_len, sub_len), sub_len)
        qd[dst, :] = (q_ref[members, :] * (B_HEAD_DIM ** -0.5)).astype(BF16)
        kd[dst, :] = k_ref[members, :].astype(BF16)
        vd[dst, :] = v_ref[members, :].astype(BF16)
        return carry

    lax.fori_loop(0, r, split, 0)

    def task(tk):
        c = tk // nq
        iq = tk % nq
        q0 = iq * lq
        ws = jnp.clip(q0 - half, 0, sub_len - win)
        case = jnp.where(iq == 0, 0, jnp.where(iq == nq - 1, 2, 1))
        base = c * sub_len
        q = qd[pl.ds(pl.multiple_of(base + q0, lq), lq), :]
        kwin = kd[pl.ds(pl.multiple_of(base + ws, half), win), :]
        vwin = vd[pl.ds(pl.multiple_of(base + ws, half), win), :]
        qm = jnp.concatenate([jnp.where(hl, q, jnp.zeros_like(q)) for hl in head_lanes], axis=0)
        bias = jnp.concatenate([bias_ref[case, h] for h in range(heads)], axis=0)
        s = _dot_nt(qm, kwin) + bias
        m = jnp.max(s, -1, keepdims=True)
        p = jnp.exp(s - m)
        den = jnp.sum(p, -1, keepdims=True)
        o2 = _dot(p.astype(BF16), vwin) / den
        lse2 = m + jnp.log(den)
        out = o2[0:lq]
        lse = jnp.broadcast_to(lse2[0:lq], (lq, lanes))
        for h in range(1, heads):
            out = jnp.where(head_lanes[h], o2[h * lq:(h + 1) * lq], out)
            lse = jnp.where(head_lanes[h], lse2[h * lq:(h + 1) * lq], lse)
        dst = pl.ds(c + q0 * r, lq, stride=r)
        oacc[dst, :] = out
        lse_ref[dst, :] = lse

    def group(i, carry):
        for u in range(unroll):
            task(i * unroll + u)
        return carry

    lax.fori_loop(0, n_tasks // unroll, group, 0)
    o_ref[...] = oacc[...].astype(o_ref.dtype)


def _dilated_group(hb, rel_bias_g, gi, window, dilation):
    bsz, seq, _ = hb.shape
    r = dilation
    sub_len = seq // r
    half = window // (2 * r)
    lq, win, nq = _att_geometry(sub_len, half)
    lanes = ATT_LANES
    pairs = B_GROUP_WIDTH // lanes
    heads = lanes // B_HEAD_DIM
    bias = _att_bias(rel_bias_g, sub_len, r, half)

    def part(p):
        return pl.BlockSpec((None, seq, lanes), lambda b, hp, p=p: (b, 0, (p * len(B_GROUPS) + gi) * pairs + hp))

    def out_spec():
        return pl.BlockSpec((None, seq, lanes), lambda b, hp: (b, 0, hp))

    body = functools.partial(_att_body, seq=seq, r=r, lq=lq, win=win, half=half)
    return pl.pallas_call(
        body,
        grid=(bsz, pairs),
        in_specs=[part(0), part(1), part(2),
                  pl.BlockSpec((3, heads, lq, win), lambda b, hp: (0, hp, 0, 0))],
        out_specs=[out_spec(), out_spec()],
        out_shape=[jax.ShapeDtypeStruct((bsz, seq, B_GROUP_WIDTH), BF16),
                   jax.ShapeDtypeStruct((bsz, seq, B_GROUP_WIDTH), F32)],
        scratch_shapes=[pltpu.VMEM((seq, lanes), BF16)] * 3 + [pltpu.VMEM((seq, lanes), F32)],
        compiler_params=_cparams("parallel", "arbitrary"),
        name=f"dilated_attn_r{r}",
    )(hb, hb, hb, bias)


def _ab_out_body(oa_ref, o1_ref, o2_ref, o3_ref, l1_ref, l2_ref, l3_ref, x_ref, w_ref, g_ref, b_ref, y_ref):
    l1, l2, l3 = l1_ref[...], l2_ref[...], l3_ref[...]
    m = jnp.maximum(jnp.maximum(l1, l2), l3)
    e1, e2, e3 = jnp.exp(l1 - m), jnp.exp(l2 - m), jnp.exp(l3 - m)
    tot = e1 + e2 + e3
    mix = _dot(oa_ref[...], w_ref[0:A_WIDTH, :])
    for gi, (o_ref, e) in enumerate(((o1_ref, e1), (o2_ref, e2), (o3_ref, e3))):
        og = (o_ref[...].astype(F32) * (e / tot)).astype(BF16)
        lo = A_WIDTH + gi * B_GROUP_WIDTH
        mix = mix + _dot(og, w_ref[lo:lo + B_GROUP_WIDTH, :])
    y_ref[...] = _layer_norm(ALPHA * x_ref[...] + mix, g_ref[...], b_ref[...])


def _ab_out(oa, outs, lses, x, w_out, ln_g, ln_b, tm):
    t = x.shape[0]
    gw = B_GROUP_WIDTH

    def rowblock(width):
        return pl.BlockSpec((tm, width), lambda i: (i, 0))

    def whole(shape):
        return pl.BlockSpec(shape, lambda i: (0, 0))

    return pl.pallas_call(
        _ab_out_body,
        grid=(t // tm,),
        in_specs=[rowblock(A_WIDTH)] + [rowblock(gw)] * 6 + [rowblock(D_MODEL), whole(w_out.shape),
                                                           whole((1, D_MODEL)), whole((1, D_MODEL))],
        out_specs=rowblock(D_MODEL),
        out_shape=jax.ShapeDtypeStruct((t, D_MODEL), F32),
        compiler_params=_cparams("parallel"),
        name="ab_out_ln",
    )(oa, *outs, *lses, x, w_out, ln_g.reshape(1, D_MODEL), ln_b.reshape(1, D_MODEL))


def _mlp_body(x_ref, w1_ref, w2_ref, g_ref, b_ref, y_ref, xb, acc):
    j = pl.program_id(1)

    def hidden(xbv):
        return jnp.square(jnp.maximum(_dot(xbv, w1_ref[...]), 0.0)).astype(BF16)

    @pl.when(j == 0)
    def _():
        x = x_ref[...]
        xbv = x.astype(BF16)
        xb[...] = xbv
        acc[...] = ALPHA * x + _dot(hidden(xbv), w2_ref[...])

    @pl.when(j > 0)
    def _():
        acc[...] += _dot(hidden(xb[...]), w2_ref[...])

    @pl.when(j == pl.num_programs(1) - 1)
    def _():
        y_ref[...] = _layer_norm(acc[...], g_ref[...], b_ref[...])


def _mlp(x, w1, w2, ln_g, ln_b, tm, tf):
    t = x.shape[0]
    return pl.pallas_call(
        _mlp_body,
        grid=(t // tm, D_FF // tf),
        in_specs=[pl.BlockSpec((tm, D_MODEL), lambda i, j: (i, 0)),
                  pl.BlockSpec((D_MODEL, tf), lambda i, j: (0, j)),
                  pl.BlockSpec((tf, D_MODEL), lambda i, j: (j, 0)),
                  pl.BlockSpec((1, D_MODEL), lambda i, j: (0, 0)),
                  pl.BlockSpec((1, D_MODEL), lambda i, j: (0, 0))],
        out_specs=pl.BlockSpec((tm, D_MODEL), lambda i, j: (i, 0)),
        out_shape=jax.ShapeDtypeStruct((t, D_MODEL), F32),
        scratch_shapes=[pltpu.VMEM((tm, D_MODEL), BF16), pltpu.VMEM((tm, D_MODEL), F32)],
        compiler_params=_cparams("parallel", "arbitrary"),
        name="mlp_ln",
    )(x, w1, w2, ln_g.reshape(1, D_MODEL), ln_b.reshape(1, D_MODEL))


def _glu_body(x_ref, w_ref, b_ref, u_ref, *, tn):
    xb = x_ref[...].astype(BF16)
    for c0 in range(0, D_MODEL, tn):
        a = _dot(xb, w_ref[:, c0:c0 + tn]) + b_ref[:, c0:c0 + tn]
        gate = _dot(xb, w_ref[:, D_MODEL + c0:D_MODEL + c0 + tn]) + b_ref[:, D_MODEL + c0:D_MODEL + c0 + tn]
        u_ref[:, c0:c0 + tn] = a * jax.nn.sigmoid(gate)


def _glu_proj(x, w_in, b_in, tm, tn):
    t = x.shape[0]
    body = functools.partial(_glu_body, tn=tn)
    return pl.pallas_call(
        body,
        grid=(t // tm,),
        in_specs=[pl.BlockSpec((tm, D_MODEL), lambda i: (i, 0)),
                  pl.BlockSpec((D_MODEL, 2 * D_MODEL), lambda i: (0, 0)),
                  pl.BlockSpec((1, 2 * D_MODEL), lambda i: (0, 0))],
        out_specs=pl.BlockSpec((tm, D_MODEL), lambda i: (i, 0)),
        out_shape=jax.ShapeDtypeStruct((t, D_MODEL), F32),
        compiler_params=_cparams("parallel"),
        name="conv_glu",
    )(x, w_in, b_in.reshape(1, 2 * D_MODEL))


def _conv_out_body(u_ref, up_ref, un_ref, x_ref, dw_ref, dwb_ref, ng_ref, nb_ref, w_ref, bo_ref, g_ref, b_ref,
                   y_ref, buf, act, *, ts):
    i = pl.program_id(1)
    halo = CONV_HALO
    pad = CONV_WIDTH // 2
    for s in range(D_MODEL // CONV_LANES):
        cs = slice(s * CONV_LANES, (s + 1) * CONV_LANES)
        buf[s, 0:halo, :] = jnp.where(i > 0, up_ref[0, :, cs], 0.0)
        buf[s, halo:halo + ts, :] = u_ref[0, :, cs]
        buf[s, halo + ts:halo + ts + halo, :] = jnp.where(i < pl.num_programs(1) - 1, un_ref[0, :, cs], 0.0)
    dwb, ng, nb = dwb_ref[...], ng_ref[...], nb_ref[...]
    n_tiles = ts // CONV_ROWS

    def conv_tile(t):
        base = pl.multiple_of(t * CONV_ROWS, CONV_ROWS)
        n_groups = CONV_ROWS // SUBLANES
        cols = []
        for s in range(D_MODEL // CONV_LANES):
            lanes = slice(s * CONV_LANES, (s + 1) * CONV_LANES)
            accs = [jnp.zeros((SUBLANES, CONV_LANES), F32) for _ in range(n_groups)]
            for o in range(halo - pad, CONV_ROWS + halo + pad - SUBLANES + 1):
                data = buf[s, pl.ds(base + o, SUBLANES, stride=1), :]
                for g in range(n_groups):
                    j = o - (halo - pad) - SUBLANES * g
                    if 0 <= j < CONV_WIDTH:
                        accs[g] = accs[g] + dw_ref[j:j + 1, lanes] * data
            cols.append(jnp.concatenate(accs, axis=0))
        c = _layer_norm(jnp.concatenate(cols, axis=1) + dwb, ng, nb)
        act[t % 2] = (c * jax.nn.sigmoid(c)).astype(BF16)

    def out_tile(t):
        rows = pl.ds(pl.multiple_of(t * CONV_ROWS, CONV_ROWS), CONV_ROWS)
        mix = _dot(act[t % 2], w_ref[...]) + bo_ref[...]
        y_ref[0, rows, :] = _layer_norm(ALPHA * x_ref[0, rows, :] + mix, g_ref[...], b_ref[...])

    conv_tile(0)

    def pipelined(t, carry):
        conv_tile(t)
        out_tile(t - 1)
        return carry

    lax.fori_loop(1, n_tiles, pipelined, 0)
    out_tile(n_tiles - 1)


def _conv_out(u, x, dw, dw_b, n_g, n_b, w_out, b_out, ln_g, ln_b, ts):
    bsz, seq, d = x.shape
    hb = ts // CONV_HALO
    last_halo = seq // CONV_HALO - 1

    def vec():
        return pl.BlockSpec((1, d), lambda b, i: (0, 0))

    body = functools.partial(_conv_out_body, ts=ts)
    return pl.pallas_call(
        body,
        grid=(bsz, seq // ts),
        in_specs=[pl.BlockSpec((1, ts, d), lambda b, i: (b, i, 0)),
                  pl.BlockSpec((1, CONV_HALO, d), lambda b, i: (b, jnp.maximum(i * hb - 1, 0), 0)),
                  pl.BlockSpec((1, CONV_HALO, d), lambda b, i: (b, jnp.minimum((i + 1) * hb, last_halo), 0)),
                  pl.BlockSpec((1, ts, d), lambda b, i: (b, i, 0)),
                  pl.BlockSpec((CONV_WIDTH, d), lambda b, i: (0, 0)),
                  vec(), vec(), vec(),
                  pl.BlockSpec((d, d), lambda b, i: (0, 0)),
                  vec(), vec(), vec()],
        out_specs=pl.BlockSpec((1, ts, d), lambda b, i: (b, i, 0)),
        out_shape=jax.ShapeDtypeStruct((bsz, seq, d), F32),
        scratch_shapes=[pltpu.VMEM((d // CONV_LANES, ts + 2 * CONV_HALO, CONV_LANES), F32),
                        pltpu.VMEM((2, CONV_ROWS, d), BF16)],
        compiler_params=_cparams("parallel", "arbitrary"),
        name="conv_out_ln",
    )(u, u, u, x, dw, dw_b.reshape(1, d), n_g.reshape(1, d), n_b.reshape(1, d), w_out,
      b_out.reshape(1, d), ln_g.reshape(1, d), ln_b.reshape(1, d))


def _tiles(tokens):
    tm = 1024 if tokens % 1024 == 0 else 256
    return tm


def _trunk(x, rel_bias, hgrn_lb, w_in_ab, hgrn_norm, w_out_ab, w_in_c, b_in_c, dw_c, dw_b_c, cnorm_g, cnorm_b,
           w_out_c, b_out_c, ln_mix_g, ln_mix_b, mlp_w1, mlp_w2, ln_ffn_g, ln_ffn_b):
    bsz, seq, d = x.shape
    t = bsz * seq
    tm = _tiles(t)
    x2 = x.reshape(t, d)
    for l in range(DEPTH):
        i = l // 2
        if l % 2 == 0:
            wa = w_in_ab[i][:, :5 * A_WIDTH].reshape(d, 5, A_HEADS, A_HEAD_DIM)
            wa = wa.transpose(0, 2, 1, 3).reshape(d, 5 * A_WIDTH).astype(BF16)
            wb = w_in_ab[i][:, 5 * A_WIDTH:].astype(BF16)
            chunk = min(HGRN_CHUNK, seq)
            hg, hb = _in_proj(x2, wa, wb, hgrn_lb, hgrn_norm[i], min(IN_PROJ_ROWS, seq), chunk, l)
            oa = _hgrn_mixer(hg.reshape(bsz, seq, HG_PLANES * A_WIDTH), chunk)
            hb3 = hb.reshape(bsz, seq, 3 * B_WIDTH)
            outs, lses = [], []
            for gi, (window, r) in enumerate(B_GROUPS):
                hs = slice(gi * B_HEADS_PER_GROUP, (gi + 1) * B_HEADS_PER_GROUP)
                o, lse = _dilated_group(hb3, rel_bias[:, hs], gi, window, r)
                outs.append(o.reshape(t, B_GROUP_WIDTH))
                lses.append(lse.reshape(t, B_GROUP_WIDTH))
            x2 = _ab_out(oa.reshape(t, A_WIDTH), outs, lses, x2, w_out_ab[i].astype(BF16),
                         ln_mix_g[l], ln_mix_b[l], tm)
        else:
            u = _glu_proj(x2, w_in_c[i].astype(BF16), b_in_c[i], min(GLU_ROWS, t), GLU_COLS)
            x2 = _conv_out(u.reshape(bsz, seq, d), x2.reshape(bsz, seq, d), dw_c[i], dw_b_c[i], cnorm_g[i],
                           cnorm_b[i], w_out_c[i].astype(BF16), b_out_c[i], ln_mix_g[l], ln_mix_b[l],
                           min(CONV_BLOCK, seq)).reshape(t, d)
        x2 = _mlp(x2, mlp_w1[l].astype(BF16), mlp_w2[l].astype(BF16), ln_ffn_g[l], ln_ffn_b[l], tm, 1024)
    return x2.reshape(bsz, seq, d)


def kernel(x_prompt, x_sample, rel_bias, hgrn_lb, w_in_ab, hgrn_norm, w_out_ab, w_in_c, b_in_c, dw_c, dw_b_c,
           cnorm_g, cnorm_b, w_out_c, b_out_c, ln_mix_g, ln_mix_b, mlp_w1, mlp_w2, ln_ffn_g, ln_ffn_b):
    params = (rel_bias, hgrn_lb, w_in_ab, hgrn_norm, w_out_ab, w_in_c, b_in_c, dw_c, dw_b_c, cnorm_g, cnorm_b,
              w_out_c, b_out_c, ln_mix_g, ln_mix_b, mlp_w1, mlp_w2, ln_ffn_g, ln_ffn_b)
    return (_trunk(x_prompt, *params), _trunk(x_sample, *params))
```

```python
import functools

import numpy as np
import jax
import jax.numpy as jnp
from jax import lax
from jax.experimental import pallas as pl
from jax.experimental.pallas import tpu as pltpu

F32 = jnp.float32
BF16 = jnp.bfloat16

D_MODEL = 1024
DEPTH = 2
A_HEADS = 4
A_HEAD_DIM = 128
A_WIDTH = A_HEADS * A_HEAD_DIM
B_GROUPS = ((128, 1), (512, 4), (2048, 16))
B_HEADS_PER_GROUP = 4
B_HEAD_DIM = 64
B_GROUP_WIDTH = B_HEADS_PER_GROUP * B_HEAD_DIM
B_WIDTH = B_GROUP_WIDTH * len(B_GROUPS)
CONV_WIDTH = 31
D_FF = 4 * D_MODEL
REL_BUCKETS = 32
REL_MAX_DIST = 1024
ALPHA = (2 * DEPTH) ** 0.25
LN_EPS = 1e-5
RMS_EPS = 1e-6
NEG_INF = -1e30

SUBLANES = 8
IN_PROJ_ROWS = 256
HGRN_CHUNK = 128
HGRN_UNROLL = 4
ATT_QBLOCK = 128
ATT_LANES = 128
ATT_UNROLL = 8
GLU_ROWS = 512
GLU_COLS = 512
CONV_BLOCK = 1024
CONV_HALO = 16
CONV_ROWS = 128
CONV_LANES = 128
VMEM_LIMIT = 52 * 1024 * 1024


def _cparams(*sem):
    return pltpu.CompilerParams(dimension_semantics=sem, vmem_limit_bytes=VMEM_LIMIT)


def _dot(a, b):
    return jnp.dot(a, b, preferred_element_type=F32)


def _dot_nt(a, b):
    return lax.dot_general(a, b, (((1,), (1,)), ((), ())), preferred_element_type=F32)


def _dot_tn(a, b):
    return lax.dot_general(a, b, (((0,), (0,)), ((), ())), preferred_element_type=F32)


def _layer_norm(y, g, b):
    mu = jnp.mean(y, -1, keepdims=True)
    yc = y - mu
    var = jnp.mean(jnp.square(yc), -1, keepdims=True)
    return yc * lax.rsqrt(var + LN_EPS) * g + b


HG_PLANES = 7


def _in_proj_body(x_ref, wa_ref, wb_ref, lbp_ref, gn_ref, hg_ref, hb_ref, ha, *, chunk, layer):
    tm = x_ref.shape[0]
    dk = A_HEAD_DIM
    xb = x_ref[...].astype(BF16)
    row = lax.broadcasted_iota(jnp.int32, (chunk, dk), 0)

    def lower_bound(direction, lanes):
        p = [lbp_ref[direction, l:l + 1, lanes] for l in range(DEPTH + 1)]
        m = functools.reduce(jnp.maximum, p)
        e = [jnp.exp(v - m) for v in p]
        return sum(e[:layer + 1]) / sum(e)

    def gates(z, lb_d):
        log2f = jnp.log2(lb_d + (1.0 - lb_d) * jax.nn.sigmoid(z))
        k = (1.0 - lb_d) * jax.nn.sigmoid(-z)
        return k, log2f

    def plane(p, h):
        return slice(p * A_WIDTH + h * dk, p * A_WIDTH + (h + 1) * dk)

    head_cols = wa_ref.shape[1] // A_HEADS
    for hp in range(A_HEADS // 2):
        ha[hp] = _dot(xb, wa_ref[:, hp * 2 * head_cols:(hp + 1) * 2 * head_cols])
    hb_ref[...] = _dot(xb, wb_ref[...])
    for h in range(A_HEADS):
        lanes = slice(h * dk, (h + 1) * dk)
        lb_f, lb_b = lower_bound(0, lanes), lower_bound(1, lanes)
        gn = gn_ref[:, lanes]

        def part(p, rows):
            c0 = (h % 2) * head_cols + p * dk
            return ha[h // 2, rows, c0:c0 + dk]

        for r0 in range(0, tm, chunk):
            rows = slice(r0, r0 + chunk)
            q = part(0, rows)
            hg_ref[rows, plane(0, h)] = q * jax.nn.sigmoid(q)
            hg_ref[rows, plane(1, h)] = part(1, rows)
            k, lf = gates(part(2, rows), lb_f)
            hg_ref[rows, plane(2, h)] = k
            s = 1
            while s < chunk:
                lf = lf + jnp.where(row >= s, pltpu.roll(lf, s, 0), 0.0)
                s *= 2
            hg_ref[rows, plane(4, h)] = lf
            k, lg = gates(part(3, rows), lb_b)
            hg_ref[rows, plane(3, h)] = k
            s = 1
            while s < chunk:
                lg = lg + jnp.where(row < chunk - s, pltpu.roll(lg, chunk - s, 0), 0.0)
                s *= 2
            hg_ref[rows, plane(5, h)] = lg
            g = part(4, rows)
            hg_ref[rows, plane(6, h)] = gn * (g * jax.nn.sigmoid(g))


def _in_proj(x, wa, wb, hgrn_lb, g_norm, tm, chunk, layer):
    m, k = x.shape
    na, nb = wa.shape[1], wb.shape[1]
    body = functools.partial(_in_proj_body, chunk=chunk, layer=layer)
    return pl.pallas_call(
        body,
        grid=(m // tm,),
        in_specs=[pl.BlockSpec((tm, k), lambda i: (i, 0)),
                  pl.BlockSpec((k, na), lambda i: (0, 0)),
                  pl.BlockSpec((k, nb), lambda i: (0, 0)),
                  pl.BlockSpec(hgrn_lb.shape, lambda i: (0, 0, 0)),
                  pl.BlockSpec((1, A_WIDTH), lambda i: (0, 0))],
        out_specs=[pl.BlockSpec((tm, HG_PLANES * A_WIDTH), lambda i: (i, 0)),
                   pl.BlockSpec((tm, nb), lambda i: (i, 0))],
        out_shape=[jax.ShapeDtypeStruct((m, HG_PLANES * A_WIDTH), F32), jax.ShapeDtypeStruct((m, nb), F32)],
        scratch_shapes=[pltpu.VMEM((A_HEADS // 2, tm, 2 * na // A_HEADS), F32)],
        compiler_params=_cparams("parallel"),
        name="in_proj",
    )(x, wa, wb, hgrn_lb, g_norm.reshape(1, A_WIDTH))


def _hgrn_body(qs, v_ref, kf, kb, bf, cb, gs, o_ref, of, ob, sf, sb, level, *, seq, chunk):
    C = chunk
    n_chunks = seq // C
    dk = A_HEAD_DIM
    grp = (C // SUBLANES, SUBLANES, dk)

    sf[...] = jnp.zeros_like(sf)
    sb[...] = jnp.zeros_like(sb)
    sub = lax.broadcasted_iota(jnp.int32, grp, 1)
    level[...] = 31 - lax.clz(lax.broadcasted_iota(jnp.int32, (C, C), 0) ^ lax.broadcasted_iota(jnp.int32, (C, C), 1))

    def mid_row(x3, w):
        if w == 1:
            return jnp.where((sub & 1) == 1, x3, pltpu.roll(x3, SUBLANES - 1, 1))
        if w == 2:
            return jnp.where(sub < 4, jnp.broadcast_to(x3[:, 2:3, :], grp), jnp.broadcast_to(x3[:, 6:7, :], grp))
        return jnp.broadcast_to(x3[:, 4:5, :], grp)

    def intra(q, kfv, kbv, bfv, cbv, v):
        lv = level[...]
        scores = jnp.zeros((C, C), F32)
        q3 = q.reshape(grp)
        kf3, kb3, bf3, cb3 = kfv.reshape(grp), kbv.reshape(grp), bfv.reshape(grp), cbv.reshape(grp)
        w = 1
        while w < C:
            if w < SUBLANES:
                second = (sub & w) != 0
                d_f = bf3 - mid_row(bf3, w)
                d_b = cb3 - mid_row(cb3, w)
                qt = (q3 * jnp.exp2(jnp.where(second, d_f, d_b))).reshape(C, dk)
                kt = (jnp.where(second, kb3, kf3) * jnp.exp2(-jnp.where(second, d_b, d_f))).reshape(C, dk)
            else:
                qparts, kparts = [], []
                for r0 in range(0, C, 2 * w):
                    first, second = slice(r0, r0 + w), slice(r0 + w, r0 + 2 * w)
                    bf_r, cb_r = bfv[r0 + w:r0 + w + 1, :], cbv[r0 + w:r0 + w + 1, :]
                    qparts += [q[first] * jnp.exp2(cbv[first] - cb_r), q[second] * jnp.exp2(bfv[second] - bf_r)]
                    kparts += [kfv[first] * jnp.exp2(bf_r - bfv[first]), kbv[second] * jnp.exp2(cb_r - cbv[second])]
                qt, kt = jnp.concatenate(qparts, 0), jnp.concatenate(kparts, 0)
            p = _dot_nt(qt.astype(BF16), kt.astype(BF16))
            scores = jnp.where(lv == w.bit_length() - 1, p, scores)
            w *= 2
        same = jnp.sum(q * (kfv + kbv), axis=-1, keepdims=True) * v
        return same + _dot(scores.astype(BF16), v.astype(BF16))

    def finish(rows, o):
        o = o * lax.rsqrt(jnp.mean(o * o, -1, keepdims=True) + RMS_EPS)
        o_ref[rows, :] = (o * gs[rows, :]).astype(o_ref.dtype)

    def step(n, carry, *, other_half_done):
        rows = pl.ds(pl.multiple_of(n * C, C), C)
        q, kfv, kbv, bfv, cbv, v = qs[rows, :], kf[rows, :], kb[rows, :], bf[rows, :], cb[rows, :], v_ref[rows, :]
        o = intra(q, kfv, kbv, bfv, cbv, v)
        st = sf[...]
        o = o + _dot_nt((q * jnp.exp2(bfv)).astype(BF16), st.astype(BF16))
        if other_half_done:
            finish(rows, o + ob[rows, :])
        else:
            of[rows, :] = o
        b_last = bfv[C - 1:C, :]
        kt = (kfv * jnp.exp2(b_last - bfv)).astype(BF16)
        sf[...] = jnp.exp2(b_last) * st + _dot_tn(v.astype(BF16), kt)

        rows = pl.ds(pl.multiple_of((n_chunks - 1 - n) * C, C), C)
        q, kbv, cbv, v = qs[rows, :], kb[rows, :], cb[rows, :], v_ref[rows, :]
        st = sb[...]
        o = _dot_nt((q * jnp.exp2(cbv)).astype(BF16), st.astype(BF16))
        if other_half_done:
            finish(rows, o + of[rows, :])
        else:
            ob[rows, :] = o
        c_first = cbv[0:1, :]
        kt = (kbv * jnp.exp2(c_first - cbv)).astype(BF16)
        sb[...] = jnp.exp2(c_first) * st + _dot_tn(v.astype(BF16), kt)
        return carry

    if n_chunks % 2 == 0:
        half = n_chunks // 2
        unroll = HGRN_UNROLL if half % HGRN_UNROLL == 0 else 1
        lax.fori_loop(0, half, functools.partial(step, other_half_done=False), 0, unroll=unroll)
        lax.fori_loop(half, n_chunks, functools.partial(step, other_half_done=True), 0, unroll=unroll)
    else:
        lax.fori_loop(0, n_chunks, functools.partial(step, other_half_done=False), 0)
        for n in range(n_chunks):
            rows = slice(n * C, (n + 1) * C)
            finish(rows, of[rows, :] + ob[rows, :])


def _hgrn_mixer(hg, chunk):
    bsz, seq, _ = hg.shape
    dk = A_HEAD_DIM

    def col(part):
        return pl.BlockSpec((None, seq, dk), lambda b, h, part=part: (b, 0, part * A_HEADS + h))

    body = functools.partial(_hgrn_body, seq=seq, chunk=chunk)
    return pl.pallas_call(
        body,
        grid=(bsz, A_HEADS),
        in_specs=[col(p) for p in range(HG_PLANES)],
        out_specs=pl.BlockSpec((None, seq, dk), lambda b, h: (b, 0, h)),
        out_shape=jax.ShapeDtypeStruct((bsz, seq, A_WIDTH), BF16),
        scratch_shapes=[pltpu.VMEM((seq, dk), F32)] * 2 + [pltpu.VMEM((dk, dk), F32)] * 2
        + [pltpu.VMEM((chunk, chunk), jnp.int32)],
        compiler_params=_cparams("parallel", "arbitrary"),
        name="hgrn2",
    )(*([hg] * HG_PLANES))


def _t5_buckets(rel):
    half = REL_BUCKETS // 2
    max_exact = half // 2
    n = np.abs(rel)
    large = max_exact + (np.log(np.maximum(n, 1) / max_exact) / np.log(REL_MAX_DIST / max_exact)
                         * (half - max_exact)).astype(np.int32)
    large = np.minimum(large, half - 1)
    return (np.where(rel > 0, half, 0) + np.where(n < max_exact, n, large)).astype(np.int32)


def _att_geometry(sub_len, half):
    lq = min(ATT_QBLOCK, sub_len)
    win = min(lq + 2 * half, sub_len)
    return lq, win, sub_len // lq


def _att_bias(rel_bias_g, sub_len, dilation, half):
    lq, win, _ = _att_geometry(sub_len, half)
    p = win + lq - 1
    tables = []
    for delta in (0, -half, lq - win):
        rel = np.arange(p + 1) - (lq - 1) + delta
        valid = (np.abs(rel) <= half) & (np.arange(p + 1) < p)
        u = jnp.where(valid[:, None], rel_bias_g[_t5_buckets(rel * dilation)], NEG_INF).T
        flat = jnp.tile(u, (1, lq))[:, :lq * p]
        tables.append(flat.reshape(-1, lq, p)[:, :, lq - 1:lq - 1 + win])
    return jnp.stack(tables).astype(F32)


def _att_body(q_ref, k_ref, v_ref, bias_ref, o_ref, lse_ref, qd, kd, vd, oacc, *, seq, r, lq, win, half):
    sub_len = seq // r
    nq = sub_len // lq
    n_tasks = r * nq
    unroll = min(ATT_UNROLL, n_tasks)
    lanes = q_ref.shape[-1]
    heads = lanes // B_HEAD_DIM
    lane = lax.broadcasted_iota(jnp.int32, (lq, lanes), 1)
    head_lanes = [(lane >= h * B_HEAD_DIM) & (lane < (h + 1) * B_HEAD_DIM) for h in range(heads)]

    def split(c, carry):
        members = pl.ds(c, sub_len, stride=r)
        dst = pl.ds(pl.multiple_of(c * sub_len, sub_len), sub_len)
        qd[dst, :] = (q_ref[members, :] * (B_HEAD_DIM ** -0.5)).astype(BF16)
        kd[dst, :] = k_ref[members, :].astype(BF16)
        vd[dst, :] = v_ref[members, :].astype(BF16)
        return carry

    lax.fori_loop(0, r, split, 0)

    def task(tk):
        c = tk // nq
        iq = tk % nq
        q0 = iq * lq
        ws = jnp.clip(q0 - half, 0, sub_len - win)
        case = jnp.where(iq == 0, 0, jnp.where(iq == nq - 1, 2, 1))
        base = c * sub_len
        q = qd[pl.ds(pl.multiple_of(base + q0, lq), lq), :]
        kwin = kd[pl.ds(pl.multiple_of(base + ws, half), win), :]
        vwin = vd[pl.ds(pl.multiple_of(base + ws, half), win), :]
        qm = jnp.concatenate([jnp.where(hl, q, jnp.zeros_like(q)) for hl in head_lanes], axis=0)
        bias = jnp.concatenate([bias_ref[case, h] for h in range(heads)], axis=0)
        s = _dot_nt(qm, kwin) + bias
        m = jnp.max(s, -1, keepdims=True)
        p = jnp.exp(s - m)
        den = jnp.sum(p, -1, keepdims=True)
        o2 = _dot(p.astype(BF16), vwin) / den
        lse2 = m + jnp.log(den)
        out = o2[0:lq]
        lse = jnp.broadcast_to(lse2[0:lq], (lq, lanes))
        for h in range(1, heads):
            out = jnp.where(head_lanes[h], o2[h * lq:(h + 1) * lq], out)
            lse = jnp.where(head_lanes[h], lse2[h * lq:(h + 1) * lq], lse)
        dst = pl.ds(c + q0 * r, lq, stride=r)
        oacc[dst, :] = out
        lse_ref[dst, :] = lse

    def group(i, carry):
        for u in range(unroll):
            task(i * unroll + u)
        return carry

    lax.fori_loop(0, n_tasks // unroll, group, 0)
    o_ref[...] = oacc[...].astype(o_ref.dtype)


def _dilated_group(hb, rel_bias_g, gi, window, dilation):
    bsz, seq, _ = hb.shape
    r = dilation
    sub_len = seq // r
    half = window // (2 * r)
    lq, win, nq = _att_geometry(sub_len, half)
    lanes = ATT_LANES
    pairs = B_GROUP_WIDTH // lanes
    heads = lanes // B_HEAD_DIM
    bias = _att_bias(rel_bias_g, sub_len, r, half)

    def part(p):
        return pl.BlockSpec((None, seq, lanes), lambda b, hp, p=p: (b, 0, (p * len(B_GROUPS) + gi) * pairs + hp))

    def out_spec():
        return pl.BlockSpec((None, seq, lanes), lambda b, hp: (b, 0, hp))

    body = functools.partial(_att_body, seq=seq, r=r, lq=lq, win=win, half=half)
    return pl.pallas_call(
        body,
        grid=(bsz, pairs),
        in_specs=[part(0), part(1), part(2),
                  pl.BlockSpec((3, heads, lq, win), lambda b, hp: (0, hp, 0, 0))],
        out_specs=[out_spec(), out_spec()],
        out_shape=[jax.ShapeDtypeStruct((bsz, seq, B_GROUP_WIDTH), BF16),
                   jax.ShapeDtypeStruct((bsz, seq, B_GROUP_WIDTH), F32)],
        scratch_shapes=[pltpu.VMEM((seq, lanes), BF16)] * 3 + [pltpu.VMEM((seq, lanes), F32)],
        compiler_params=_cparams("parallel", "arbitrary"),
        name=f"dilated_attn_r{r}",
    )(hb, hb, hb, bias)


def _ab_out_body(oa_ref, o1_ref, o2_ref, o3_ref, l1_ref, l2_ref, l3_ref, x_ref, w_ref, g_ref, b_ref, y_ref):
    l1, l2, l3 = l1_ref[...], l2_ref[...], l3_ref[...]
    m = jnp.maximum(jnp.maximum(l1, l2), l3)
    e1, e2, e3 = jnp.exp(l1 - m), jnp.exp(l2 - m), jnp.exp(l3 - m)
    tot = e1 + e2 + e3
    mix = _dot(oa_ref[...], w_ref[0:A_WIDTH, :])
    for gi, (o_ref, e) in enumerate(((o1_ref, e1), (o2_ref, e2), (o3_ref, e3))):
        og = (o_ref[...].astype(F32) * (e / tot)).astype(BF16)
        lo = A_WIDTH + gi * B_GROUP_WIDTH
        mix = mix + _dot(og, w_ref[lo:lo + B_GROUP_WIDTH, :])
    y_ref[...] = _layer_norm(ALPHA * x_ref[...] + mix, g_ref[...], b_ref[...])


def _ab_out(oa, outs, lses, x, w_out, ln_g, ln_b, tm):
    t = x.shape[0]
    gw = B_GROUP_WIDTH

    def rowblock(width):
        return pl.BlockSpec((tm, width), lambda i: (i, 0))

    def whole(shape):
        return pl.BlockSpec(shape, lambda i: (0, 0))

    return pl.pallas_call(
        _ab_out_body,
        grid=(t // tm,),
        in_specs=[rowblock(A_WIDTH)] + [rowblock(gw)] * 6 + [rowblock(D_MODEL), whole(w_out.shape),
                                                           whole((1, D_MODEL)), whole((1, D_MODEL))],
        out_specs=rowblock(D_MODEL),
        out_shape=jax.ShapeDtypeStruct((t, D_MODEL), F32),
        compiler_params=_cparams("parallel"),
        name="ab_out_ln",
    )(oa, *outs, *lses, x, w_out, ln_g.reshape(1, D_MODEL), ln_b.reshape(1, D_MODEL))


def _mlp_body(x_ref, w1_ref, w2_ref, g_ref, b_ref, y_ref, xb, acc):
    j = pl.program_id(1)

    def hidden(xbv):
        return jnp.square(jnp.maximum(_dot(xbv, w1_ref[...]), 0.0)).astype(BF16)

    @pl.when(j == 0)
    def _():
        x = x_ref[...]
        xbv = x.astype(BF16)
        xb[...] = xbv
        acc[...] = ALPHA * x + _dot(hidden(xbv), w2_ref[...])

    @pl.when(j > 0)
    def _():
        acc[...] += _dot(hidden(xb[...]), w2_ref[...])

    @pl.when(j == pl.num_programs(1) - 1)
    def _():
        y_ref[...] = _layer_norm(acc[...], g_ref[...], b_ref[...])


def _mlp(x, w1, w2, ln_g, ln_b, tm, tf):
    t = x.shape[0]
    return pl.pallas_call(
        _mlp_body,
        grid=(t // tm, D_FF // tf),
        in_specs=[pl.BlockSpec((tm, D_MODEL), lambda i, j: (i, 0)),
                  pl.BlockSpec((D_MODEL, tf), lambda i, j: (0, j)),
                  pl.BlockSpec((tf, D_MODEL), lambda i, j: (j, 0)),
                  pl.BlockSpec((1, D_MODEL), lambda i, j: (0, 0)),
                  pl.BlockSpec((1, D_MODEL), lambda i, j: (0, 0))],
        out_specs=pl.BlockSpec((tm, D_MODEL), lambda i, j: (i, 0)),
        out_shape=jax.ShapeDtypeStruct((t, D_MODEL), F32),
        scratch_shapes=[pltpu.VMEM((tm, D_MODEL), BF16), pltpu.VMEM((tm, D_MODEL), F32)],
        compiler_params=_cparams("parallel", "arbitrary"),
        name="mlp_ln",
    )(x, w1, w2, ln_g.reshape(1, D_MODEL), ln_b.reshape(1, D_MODEL))


def _glu_body(x_ref, w_ref, b_ref, u_ref, *, tn):
    xb = x_ref[...].astype(BF16)
    for c0 in range(0, D_MODEL, tn):
        a = _dot(xb, w_ref[:, c0:c0 + tn]) + b_ref[:, c0:c0 + tn]
        gate = _dot(xb, w_ref[:, D_MODEL + c0:D_MODEL + c0 + tn]) + b_ref[:, D_MODEL + c0:D_MODEL + c0 + tn]
        u_ref[:, c0:c0 + tn] = a * jax.nn.sigmoid(gate)


def _glu_proj(x, w_in, b_in, tm, tn):
    t = x.shape[0]
    body = functools.partial(_glu_body, tn=tn)
    return pl.pallas_call(
        body,
        grid=(t // tm,),
        in_specs=[pl.BlockSpec((tm, D_MODEL), lambda i: (i, 0)),
                  pl.BlockSpec((D_MODEL, 2 * D_MODEL), lambda i: (0, 0)),
                  pl.BlockSpec((1, 2 * D_MODEL), lambda i: (0, 0))],
        out_specs=pl.BlockSpec((tm, D_MODEL), lambda i: (i, 0)),
        out_shape=jax.ShapeDtypeStruct((t, D_MODEL), F32),
        compiler_params=_cparams("parallel"),
        name="conv_glu",
    )(x, w_in, b_in.reshape(1, 2 * D_MODEL))


def _conv_out_body(u_ref, up_ref, un_ref, x_ref, dw_ref, dwb_ref, ng_ref, nb_ref, w_ref, bo_ref, g_ref, b_ref,
                   y_ref, buf, act, *, ts):
    i = pl.program_id(1)
    halo = CONV_HALO
    pad = CONV_WIDTH // 2
    for s in range(D_MODEL // CONV_LANES):
        cs = slice(s * CONV_LANES, (s + 1) * CONV_LANES)
        buf[s, 0:halo, :] = jnp.where(i > 0, up_ref[0, :, cs], 0.0)
        buf[s, halo:halo + ts, :] = u_ref[0, :, cs]
        buf[s, halo + ts:halo + ts + halo, :] = jnp.where(i < pl.num_programs(1) - 1, un_ref[0, :, cs], 0.0)
    dwb, ng, nb = dwb_ref[...], ng_ref[...], nb_ref[...]
    n_tiles = ts // CONV_ROWS

    def conv_tile(t):
        base = pl.multiple_of(t * CONV_ROWS, CONV_ROWS)
        n_groups = CONV_ROWS // SUBLANES
        cols = []
        for s in range(D_MODEL // CONV_LANES):
            lanes = slice(s * CONV_LANES, (s + 1) * CONV_LANES)
            accs = [jnp.zeros((SUBLANES, CONV_LANES), F32) for _ in range(n_groups)]
            for o in range(halo - pad, CONV_ROWS + halo + pad - SUBLANES + 1):
                data = buf[s, pl.ds(base + o, SUBLANES, stride=1), :]
                for g in range(n_groups):
                    j = o - (halo - pad) - SUBLANES * g
                    if 0 <= j < CONV_WIDTH:
                        accs[g] = accs[g] + dw_ref[j:j + 1, lanes] * data
            cols.append(jnp.concatenate(accs, axis=0))
        c = _layer_norm(jnp.concatenate(cols, axis=1) + dwb, ng, nb)
        act[t % 2] = (c * jax.nn.sigmoid(c)).astype(BF16)

    def out_tile(t):
        rows = pl.ds(pl.multiple_of(t * CONV_ROWS, CONV_ROWS), CONV_ROWS)
        mix = _dot(act[t % 2], w_ref[...]) + bo_ref[...]
        y_ref[0, rows, :] = _layer_norm(ALPHA * x_ref[0, rows, :] + mix, g_ref[...], b_ref[...])

    conv_tile(0)

    def pipelined(t, carry):
        conv_tile(t)
        out_tile(t - 1)
        return carry

    lax.fori_loop(1, n_tiles, pipelined, 0)
    out_tile(n_tiles - 1)


def _conv_out(u, x, dw, dw_b, n_g, n_b, w_out, b_out, ln_g, ln_b, ts):
    bsz, seq, d = x.shape
    hb = ts // CONV_HALO
    last_halo = seq // CONV_HALO - 1

    def vec():
        return pl.BlockSpec((1, d), lambda b, i: (0, 0))

    body = functools.partial(_conv_out_body, ts=ts)
    return pl.pallas_call(
        body,
        grid=(bsz, seq // ts),
        in_specs=[pl.BlockSpec((1, ts, d), lambda b, i: (b, i, 0)),
                  pl.BlockSpec((1, CONV_HALO, d), lambda b, i: (b, jnp.maximum(i * hb - 1, 0), 0)),
                  pl.BlockSpec((1, CONV_HALO, d), lambda b, i: (b, jnp.minimum((i + 1) * hb, last_halo), 0)),
                  pl.BlockSpec((1, ts, d), lambda b, i: (b, i, 0)),
                  pl.BlockSpec((CONV_WIDTH, d), lambda b, i: (0, 0)),
                  vec(), vec(), vec(),
                  pl.BlockSpec((d, d), lambda b, i: (0, 0)),
                  vec(), vec(), vec()],
        out_specs=pl.BlockSpec((1, ts, d), lambda b, i: (b, i, 0)),
        out_shape=jax.ShapeDtypeStruct((bsz, seq, d), F32),
        scratch_shapes=[pltpu.VMEM((d // CONV_LANES, ts + 2 * CONV_HALO, CONV_LANES), F32),
                        pltpu.VMEM((2, CONV_ROWS, d), BF16)],
        compiler_params=_cparams("parallel", "arbitrary"),
        name="conv_out_ln",
    )(u, u, u, x, dw, dw_b.reshape(1, d), n_g.reshape(1, d), n_b.reshape(1, d), w_out,
      b_out.reshape(1, d), ln_g.reshape(1, d), ln_b.reshape(1, d))


def _tiles(tokens):
    tm = 1024 if tokens % 1024 == 0 else 256
    return tm


def _trunk(x, rel_bias, hgrn_lb, w_in_ab, hgrn_norm, w_out_ab, w_in_c, b_in_c, dw_c, dw_b_c, cnorm_g, cnorm_b,
           w_out_c, b_out_c, ln_mix_g, ln_mix_b, mlp_w1, mlp_w2, ln_ffn_g, ln_ffn_b):
    bsz, seq, d = x.shape
    t = bsz * seq
    tm = _tiles(t)
    x2 = x.reshape(t, d)
    for l in range(DEPTH):
        i = l // 2
        if l % 2 == 0:
            wa = w_in_ab[i][:, :5 * A_WIDTH].reshape(d, 5, A_HEADS, A_HEAD_DIM)
            wa = wa.transpose(0, 2, 1, 3).reshape(d, 5 * A_WIDTH).astype(BF16)
            wb = w_in_ab[i][:, 5 * A_WIDTH:].astype(BF16)
            chunk = min(HGRN_CHUNK, seq)
            hg, hb = _in_proj(x2, wa, wb, hgrn_lb, hgrn_norm[i], min(IN_PROJ_ROWS, seq), chunk, l)
            oa = _hgrn_mixer(hg.reshape(bsz, seq, HG_PLANES * A_WIDTH), chunk)
            hb3 = hb.reshape(bsz, seq, 3 * B_WIDTH)
            outs, lses = [], []
            for gi, (window, r) in enumerate(B_GROUPS):
                hs = slice(gi * B_HEADS_PER_GROUP, (gi + 1) * B_HEADS_PER_GROUP)
                o, lse = _dilated_group(hb3, rel_bias[:, hs], gi, window, r)
                outs.append(o.reshape(t, B_GROUP_WIDTH))
                lses.append(lse.reshape(t, B_GROUP_WIDTH))
            x2 = _ab_out(oa.reshape(t, A_WIDTH), outs, lses, x2, w_out_ab[i].astype(BF16),
                         ln_mix_g[l], ln_mix_b[l], tm)
        else:
            u = _glu_proj(x2, w_in_c[i].astype(BF16), b_in_c[i], min(GLU_ROWS, t), GLU_COLS)
            x2 = _conv_out(u.reshape(bsz, seq, d), x2.reshape(bsz, seq, d), dw_c[i], dw_b_c[i], cnorm_g[i],
                           cnorm_b[i], w_out_c[i].astype(BF16), b_out_c[i], ln_mix_g[l], ln_mix_b[l],
                           min(CONV_BLOCK, seq)).reshape(t, d)
        x2 = _mlp(x2, mlp_w1[l].astype(BF16), mlp_w2[l].astype(BF16), ln_ffn_g[l], ln_ffn_b[l], tm, 1024)
    return x2.reshape(bsz, seq, d)


def kernel(x_prompt, x_sample, rel_bias, hgrn_lb, w_in_ab, hgrn_norm, w_out_ab, w_in_c, b_in_c, dw_c, dw_b_c,
           cnorm_g, cnorm_b, w_out_c, b_out_c, ln_mix_g, ln_mix_b, mlp_w1, mlp_w2, ln_ffn_g, ln_ffn_b):
    params = (rel_bias, hgrn_lb, w_in_ab, hgrn_norm, w_out_ab, w_in_c, b_in_c, dw_c, dw_b_c, cnorm_g, cnorm_b,
              w_out_c, b_out_c, ln_mix_g, ln_mix_b, mlp_w1, mlp_w2, ln_ffn_g, ln_ffn_b)
    return (_trunk(x_prompt, *params), _trunk(x_sample, *params))
```

```python
import functools

import numpy as np
import jax
import jax.numpy as jnp
from jax import lax
from jax.experimental import pallas as pl
from jax.experimental.pallas import tpu as pltpu

F32 = jnp.float32
BF16 = jnp.bfloat16

D_MODEL = 1024
DEPTH = 2
A_HEADS = 4
A_HEAD_DIM = 128
A_WIDTH = A_HEADS * A_HEAD_DIM
B_GROUPS = ((128, 1), (512, 4), (2048, 16))
B_HEADS_PER_GROUP = 4
B_HEAD_DIM = 64
B_GROUP_WIDTH = B_HEADS_PER_GROUP * B_HEAD_DIM
B_WIDTH = B_GROUP_WIDTH * len(B_GROUPS)
CONV_WIDTH = 31
D_FF = 4 * D_MODEL
REL_BUCKETS = 32
REL_MAX_DIST = 1024
ALPHA = (2 * DEPTH) ** 0.25
LN_EPS = 1e-5
RMS_EPS = 1e-6
NEG_INF = -1e30

SUBLANES = 8
IN_PROJ_ROWS = 512
HGRN_CHUNK = 128
HGRN_UNROLL = 4
ATT_QBLOCK = 128
ATT_LANES = 128
ATT_UNROLL = 8
MLP_ROWS = 1024
MLP_SUB_ROWS = 256
GLU_ROWS = 1024
GLU_COLS = 512
CONV_BLOCK = 1024
CONV_HALO = 16
CONV_ROWS = 128
CONV_LANES = 128
VMEM_LIMIT = 52 * 1024 * 1024


def _cparams(*sem):
    return pltpu.CompilerParams(dimension_semantics=sem, vmem_limit_bytes=VMEM_LIMIT)


def _resident(shape):
    return pl.BlockSpec(shape, lambda *_: (0,) * len(shape), pipeline_mode=pl.Buffered(1))


def _dot(a, b):
    return jnp.dot(a, b, preferred_element_type=F32)


def _dot_nt(a, b):
    return lax.dot_general(a, b, (((1,), (1,)), ((), ())), preferred_element_type=F32)


def _dot_tn(a, b):
    return lax.dot_general(a, b, (((0,), (0,)), ((), ())), preferred_element_type=F32)


def _layer_norm(y, g, b):
    mu = jnp.mean(y, -1, keepdims=True)
    yc = y - mu
    var = jnp.mean(jnp.square(yc), -1, keepdims=True)
    return yc * lax.rsqrt(var + LN_EPS) * g + b


HG_PLANES = 7


def _in_proj_body(x_ref, wa_ref, wb_ref, lbp_ref, gn_ref, hg_ref, hb_ref, ha, *, chunk, layer):
    tm = x_ref.shape[0]
    dk = A_HEAD_DIM
    xb = x_ref[...].astype(BF16)
    row = lax.broadcasted_iota(jnp.int32, (chunk, dk), 0)

    def lower_bound(direction, lanes):
        p = [lbp_ref[direction, l:l + 1, lanes] for l in range(DEPTH + 1)]
        m = functools.reduce(jnp.maximum, p)
        e = [jnp.exp(v - m) for v in p]
        return sum(e[:layer + 1]) / sum(e)

    def gates(z, lb_d):
        log2f = jnp.log2(lb_d + (1.0 - lb_d) * jax.nn.sigmoid(z))
        k = (1.0 - lb_d) * jax.nn.sigmoid(-z)
        return k, log2f

    def plane(p, h):
        return slice(p * A_WIDTH + h * dk, p * A_WIDTH + (h + 1) * dk)

    head_cols = wa_ref.shape[1] // A_HEADS
    for hp in range(A_HEADS // 2):
        ha[hp] = _dot(xb, wa_ref[:, hp * 2 * head_cols:(hp + 1) * 2 * head_cols])
    hb_ref[...] = _dot(xb, wb_ref[...])
    for h in range(A_HEADS):
        lanes = slice(h * dk, (h + 1) * dk)
        lb_f, lb_b = lower_bound(0, lanes), lower_bound(1, lanes)
        gn = gn_ref[:, lanes]

        def part(p, rows):
            c0 = (h % 2) * head_cols + p * dk
            return ha[h // 2, rows, c0:c0 + dk]

        for r0 in range(0, tm, chunk):
            rows = slice(r0, r0 + chunk)
            q = part(0, rows)
            hg_ref[rows, plane(0, h)] = q * jax.nn.sigmoid(q)
            hg_ref[rows, plane(1, h)] = part(1, rows)
            k, lf = gates(part(2, rows), lb_f)
            hg_ref[rows, plane(2, h)] = k
            s = 1
            while s < chunk:
                lf = lf + jnp.where(row >= s, pltpu.roll(lf, s, 0), 0.0)
                s *= 2
            hg_ref[rows, plane(4, h)] = lf
            k, lg = gates(part(3, rows), lb_b)
            hg_ref[rows, plane(3, h)] = k
            s = 1
            while s < chunk:
                lg = lg + jnp.where(row < chunk - s, pltpu.roll(lg, chunk - s, 0), 0.0)
                s *= 2
            hg_ref[rows, plane(5, h)] = lg
            g = part(4, rows)
            hg_ref[rows, plane(6, h)] = gn * (g * jax.nn.sigmoid(g))


def _in_proj(x, wa, wb, hgrn_lb, g_norm, tm, chunk, layer):
    m, k = x.shape
    na, nb = wa.shape[1], wb.shape[1]
    body = functools.partial(_in_proj_body, chunk=chunk, layer=layer)
    return pl.pallas_call(
        body,
        grid=(m // tm,),
        in_specs=[pl.BlockSpec((tm, k), lambda i: (i, 0)),
                  _resident((k, na)), _resident((k, nb)), _resident(hgrn_lb.shape), _resident((1, A_WIDTH))],
        out_specs=[pl.BlockSpec((tm, HG_PLANES * A_WIDTH), lambda i: (i, 0)),
                   pl.BlockSpec((tm, nb), lambda i: (i, 0))],
        out_shape=[jax.ShapeDtypeStruct((m, HG_PLANES * A_WIDTH), F32), jax.ShapeDtypeStruct((m, nb), F32)],
        scratch_shapes=[pltpu.VMEM((A_HEADS // 2, tm, 2 * na // A_HEADS), F32)],
        compiler_params=_cparams("parallel"),
        name="in_proj",
    )(x, wa, wb, hgrn_lb, g_norm.reshape(1, A_WIDTH))


def _hgrn_body(qs, v_ref, kf, kb, bf, cb, gs, o_ref, of, ob, sf, sb, level, *, seq, chunk):
    C = chunk
    n_chunks = seq // C
    dk = A_HEAD_DIM
    grp = (C // SUBLANES, SUBLANES, dk)

    sf[...] = jnp.zeros_like(sf)
    sb[...] = jnp.zeros_like(sb)
    sub = lax.broadcasted_iota(jnp.int32, grp, 1)
    level[...] = 31 - lax.clz(lax.broadcasted_iota(jnp.int32, (C, C), 0) ^ lax.broadcasted_iota(jnp.int32, (C, C), 1))

    def mid_row(x3, w):
        if w == 1:
            return jnp.where((sub & 1) == 1, x3, pltpu.roll(x3, SUBLANES - 1, 1))
        if w == 2:
            return jnp.where(sub < 4, jnp.broadcast_to(x3[:, 2:3, :], grp), jnp.broadcast_to(x3[:, 6:7, :], grp))
        return jnp.broadcast_to(x3[:, 4:5, :], grp)

    def intra(q, kfv, kbv, bfv, cbv, v):
        lv = level[...]
        scores = jnp.zeros((C, C), F32)
        q3 = q.reshape(grp)
        kf3, kb3, bf3, cb3 = kfv.reshape(grp), kbv.reshape(grp), bfv.reshape(grp), cbv.reshape(grp)
        w = 1
        while w < C:
            if w < SUBLANES:
                second = (sub & w) != 0
                d_f = bf3 - mid_row(bf3, w)
                d_b = cb3 - mid_row(cb3, w)
                qt = (q3 * jnp.exp2(jnp.where(second, d_f, d_b))).reshape(C, dk)
                kt = (jnp.where(second, kb3, kf3) * jnp.exp2(-jnp.where(second, d_b, d_f))).reshape(C, dk)
            else:
                qparts, kparts = [], []
                for r0 in range(0, C, 2 * w):
                    first, second = slice(r0, r0 + w), slice(r0 + w, r0 + 2 * w)
                    bf_r, cb_r = bfv[r0 + w:r0 + w + 1, :], cbv[r0 + w:r0 + w + 1, :]
                    qparts += [q[first] * jnp.exp2(cbv[first] - cb_r), q[second] * jnp.exp2(bfv[second] - bf_r)]
                    kparts += [kfv[first] * jnp.exp2(bf_r - bfv[first]), kbv[second] * jnp.exp2(cb_r - cbv[second])]
                qt, kt = jnp.concatenate(qparts, 0), jnp.concatenate(kparts, 0)
            p = _dot_nt(qt.astype(BF16), kt.astype(BF16))
            scores = jnp.where(lv == w.bit_length() - 1, p, scores)
            w *= 2
        same = jnp.sum(q * (kfv + kbv), axis=-1, keepdims=True) * v
        return same + _dot(scores.astype(BF16), v.astype(BF16))

    def finish(rows, o):
        o = o * lax.rsqrt(jnp.mean(o * o, -1, keepdims=True) + RMS_EPS)
        o_ref[rows, :] = (o * gs[rows, :]).astype(o_ref.dtype)

    def step(n, carry, *, other_half_done):
        rows = pl.ds(pl.multiple_of(n * C, C), C)
        q, kfv, kbv, bfv, cbv, v = qs[rows, :], kf[rows, :], kb[rows, :], bf[rows, :], cb[rows, :], v_ref[rows, :]
        o = intra(q, kfv, kbv, bfv, cbv, v)
        st = sf[...]
        o = o + _dot_nt((q * jnp.exp2(bfv)).astype(BF16), st.astype(BF16))
        if other_half_done:
            finish(rows, o + ob[rows, :])
        else:
            of[rows, :] = o
        b_last = bfv[C - 1:C, :]
        kt = (kfv * jnp.exp2(b_last - bfv)).astype(BF16)
        sf[...] = jnp.exp2(b_last) * st + _dot_tn(v.astype(BF16), kt)

        rows = pl.ds(pl.multiple_of((n_chunks - 1 - n) * C, C), C)
        q, kbv, cbv, v = qs[rows, :], kb[rows, :], cb[rows, :], v_ref[rows, :]
        st = sb[...]
        o = _dot_nt((q * jnp.exp2(cbv)).astype(BF16), st.astype(BF16))
        if other_half_done:
            finish(rows, o + of[rows, :])
        else:
            ob[rows, :] = o
        c_first = cbv[0:1, :]
        kt = (kbv * jnp.exp2(c_first - cbv)).astype(BF16)
        sb[...] = jnp.exp2(c_first) * st + _dot_tn(v.astype(BF16), kt)
        return carry

    if n_chunks % 2 == 0:
        half = n_chunks // 2
        unroll = HGRN_UNROLL if half % HGRN_UNROLL == 0 else 1
        lax.fori_loop(0, half, functools.partial(step, other_half_done=False), 0, unroll=unroll)
        lax.fori_loop(half, n_chunks, functools.partial(step, other_half_done=True), 0, unroll=unroll)
    else:
        lax.fori_loop(0, n_chunks, functools.partial(step, other_half_done=False), 0)
        for n in range(n_chunks):
            rows = slice(n * C, (n + 1) * C)
            finish(rows, of[rows, :] + ob[rows, :])


def _hgrn_mixer(hg, chunk):
    bsz, seq, _ = hg.shape
    dk = A_HEAD_DIM

    def col(part):
        return pl.BlockSpec((None, seq, dk), lambda b, h, part=part: (b, 0, part * A_HEADS + h))

    body = functools.partial(_hgrn_body, seq=seq, chunk=chunk)
    return pl.pallas_call(
        body,
        grid=(bsz, A_HEADS),
        in_specs=[col(p) for p in range(HG_PLANES)],
        out_specs=pl.BlockSpec((None, seq, dk), lambda b, h: (b, 0, h)),
        out_shape=jax.ShapeDtypeStruct((bsz, seq, A_WIDTH), BF16),
        scratch_shapes=[pltpu.VMEM((seq, dk), F32)] * 2 + [pltpu.VMEM((dk, dk), F32)] * 2
        + [pltpu.VMEM((chunk, chunk), jnp.int32)],
        compiler_params=_cparams("parallel", "arbitrary"),
        name="hgrn2",
    )(*([hg] * HG_PLANES))


def _t5_buckets(rel):
    half = REL_BUCKETS // 2
    max_exact = half // 2
    n = np.abs(rel)
    large = max_exact + (np.log(np.maximum(n, 1) / max_exact) / np.log(REL_MAX_DIST / max_exact)
                         * (half - max_exact)).astype(np.int32)
    large = np.minimum(large, half - 1)
    return (np.where(rel > 0, half, 0) + np.where(n < max_exact, n, large)).astype(np.int32)


def _att_geometry(sub_len, half):
    lq = min(ATT_QBLOCK, sub_len)
    win = min(lq + 2 * half, sub_len)
    return lq, win, sub_len // lq


def _att_bias(rel_bias_g, sub_len, dilation, half):
    lq, win, _ = _att_geometry(sub_len, half)
    p = win + lq - 1
    tables = []
    for delta in (0, -half, lq - win):
        rel = np.arange(p + 1) - (lq - 1) + delta
        valid = (np.abs(rel) <= half) & (np.arange(p + 1) < p)
        u = jnp.where(valid[:, None], rel_bias_g[_t5_buckets(rel * dilation)], NEG_INF).T
        flat = jnp.tile(u, (1, lq))[:, :lq * p]
        tables.append(flat.reshape(-1, lq, p)[:, :, lq - 1:lq - 1 + win])
    return jnp.stack(tables).astype(F32)


def _att_body(q_ref, k_ref, v_ref, bias_ref, o_ref, lse_ref, qd, kd, vd, oacc, *, seq, r, lq, win, half):
    sub_len = seq // r
    nq = sub_len // lq
    n_tasks = r * nq
    unroll = min(ATT_UNROLL, n_tasks)
    lanes = q_ref.shape[-1]
    heads = lanes // B_HEAD_DIM
    lane = lax.broadcasted_iota(jnp.int32, (lq, lanes), 1)
    head_lanes = [(lane >= h * B_HEAD_DIM) & (lane < (h + 1) * B_HEAD_DIM) for h in range(heads)]

    def split(c, carry):
        members = pl.ds(c, sub_len, stride=r)
        dst = pl.ds(pl.multiple_of(c * sub_len, sub_len), sub_len)
        qd[dst, :] = (q_ref[members, :] * (B_HEAD_DIM ** -0.5)).astype(BF16)
        kd[dst, :] = k_ref[members, :].astype(BF16)
        vd[dst, :] = v_ref[members, :].astype(BF16)
        return carry

    lax.fori_loop(0, r, split, 0)

    def task(tk):
        c = tk // nq
        iq = tk % nq
        q0 = iq * lq
        ws = jnp.clip(q0 - half, 0, sub_len - win)
        case = jnp.where(iq == 0, 0, jnp.where(iq == nq - 1, 2, 1))
        base = c * sub_len
        q = qd[pl.ds(pl.multiple_of(base + q0, lq), lq), :]
        kwin = kd[pl.ds(pl.multiple_of(base + ws, half), win), :]
        vwin = vd[pl.ds(pl.multiple_of(base + ws, half), win), :]
        qm = jnp.concatenate([jnp.where(hl, q, jnp.zeros_like(q)) for hl in head_lanes], axis=0)
        bias = jnp.concatenate([bias_ref[case, h] for h in range(heads)], axis=0)
        s = _dot_nt(qm, kwin) + bias
        m = jnp.max(s, -1, keepdims=True)
        p = jnp.exp(s - m)
        den = jnp.sum(p, -1, keepdims=True)
        o2 = _dot(p.astype(BF16), vwin) / den
        lse2 = m + jnp.log(den)
        out = o2[0:lq]
        lse = jnp.broadcast_to(lse2[0:lq], (lq, lanes))
        for h in range(1, heads):
            out = jnp.where(head_lanes[h], o2[h * lq:(h + 1) * lq], out)
            lse = jnp.where(head_lanes[h], lse2[h * lq:(h + 1) * lq], lse)
        dst = pl.ds(c + q0 * r, lq, stride=r)
        oacc[dst, :] = out
        lse_ref[dst, :] = lse

    def group(i, carry):
        for u in range(unroll):
            task(i * unroll + u)
        return carry

    lax.fori_loop(0, n_tasks // unroll, group, 0)
    o_ref[...] = oacc[...].astype(o_ref.dtype)


def _dilated_group(hb, rel_bias_g, gi, window, dilation):
    bsz, seq, _ = hb.shape
    r = dilation
    sub_len = seq // r
    half = window // (2 * r)
    lq, win, nq = _att_geometry(sub_len, half)
    lanes = ATT_LANES
    pairs = B_GROUP_WIDTH // lanes
    heads = lanes // B_HEAD_DIM
    bias = _att_bias(rel_bias_g, sub_len, r, half)

    def part(p):
        return pl.BlockSpec((None, seq, lanes), lambda b, hp, p=p: (b, 0, (p * len(B_GROUPS) + gi) * pairs + hp))

    def out_spec():
        return pl.BlockSpec((None, seq, lanes), lambda b, hp: (b, 0, hp))

    body = functools.partial(_att_body, seq=seq, r=r, lq=lq, win=win, half=half)
    return pl.pallas_call(
        body,
        grid=(bsz, pairs),
        in_specs=[part(0), part(1), part(2),
                  pl.BlockSpec((3, heads, lq, win), lambda b, hp: (0, hp, 0, 0))],
        out_specs=[out_spec(), out_spec()],
        out_shape=[jax.ShapeDtypeStruct((bsz, seq, B_GROUP_WIDTH), BF16),
                   jax.ShapeDtypeStruct((bsz, seq, B_GROUP_WIDTH), F32)],
        scratch_shapes=[pltpu.VMEM((seq, lanes), BF16)] * 3 + [pltpu.VMEM((seq, lanes), F32)],
        compiler_params=_cparams("parallel", "arbitrary"),
        name=f"dilated_attn_r{r}",
    )(hb, hb, hb, bias)


def _ab_out_body(oa_ref, o1_ref, o2_ref, o3_ref, l1_ref, l2_ref, l3_ref, x_ref, w_ref, g_ref, b_ref, y_ref):
    l1, l2, l3 = l1_ref[...], l2_ref[...], l3_ref[...]
    m = jnp.maximum(jnp.maximum(l1, l2), l3)
    e1, e2, e3 = jnp.exp(l1 - m), jnp.exp(l2 - m), jnp.exp(l3 - m)
    tot = e1 + e2 + e3
    mix = _dot(oa_ref[...], w_ref[0:A_WIDTH, :])
    for gi, (o_ref, e) in enumerate(((o1_ref, e1), (o2_ref, e2), (o3_ref, e3))):
        og = (o_ref[...].astype(F32) * (e / tot)).astype(BF16)
        lo = A_WIDTH + gi * B_GROUP_WIDTH
        mix = mix + _dot(og, w_ref[lo:lo + B_GROUP_WIDTH, :])
    y_ref[...] = _layer_norm(ALPHA * x_ref[...] + mix, g_ref[...], b_ref[...])


def _ab_out(oa, outs, lses, x, w_out, ln_g, ln_b, tm):
    t = x.shape[0]
    gw = B_GROUP_WIDTH

    def rowblock(width):
        return pl.BlockSpec((tm, width), lambda i: (i, 0))

    return pl.pallas_call(
        _ab_out_body,
        grid=(t // tm,),
        in_specs=[rowblock(A_WIDTH)] + [rowblock(gw)] * 6 + [rowblock(D_MODEL), _resident(w_out.shape),
                                                           _resident((1, D_MODEL)), _resident((1, D_MODEL))],
        out_specs=rowblock(D_MODEL),
        out_shape=jax.ShapeDtypeStruct((t, D_MODEL), F32),
        compiler_params=_cparams("parallel"),
        name="ab_out_ln",
    )(oa, *outs, *lses, x, w_out, ln_g.reshape(1, D_MODEL), ln_b.reshape(1, D_MODEL))


def _mlp_body(x_ref, w1_ref, w2_ref, g_ref, b_ref, y_ref, *, sub):
    for r0 in range(0, x_ref.shape[0], sub):
        rows = slice(r0, r0 + sub)
        x = x_ref[rows, :]
        h = jnp.square(jnp.maximum(_dot(x.astype(BF16), w1_ref[...]), 0.0)).astype(BF16)
        y_ref[rows, :] = _layer_norm(ALPHA * x + _dot(h, w2_ref[...]), g_ref[...], b_ref[...])


def _mlp(x, w1, w2, ln_g, ln_b, tm, sub):
    t = x.shape[0]
    body = functools.partial(_mlp_body, sub=sub)
    return pl.pallas_call(
        body,
        grid=(t // tm,),
        in_specs=[pl.BlockSpec((tm, D_MODEL), lambda i: (i, 0)),
                  _resident(w1.shape), _resident(w2.shape),
                  _resident((1, D_MODEL)), _resident((1, D_MODEL))],
        out_specs=pl.BlockSpec((tm, D_MODEL), lambda i: (i, 0)),
        out_shape=jax.ShapeDtypeStruct((t, D_MODEL), F32),
        compiler_params=_cparams("parallel"),
        name="mlp_ln",
    )(x, w1, w2, ln_g.reshape(1, D_MODEL), ln_b.reshape(1, D_MODEL))


def _glu_body(x_ref, w_ref, b_ref, u_ref, *, tn):
    xb = x_ref[...].astype(BF16)
    for c0 in range(0, D_MODEL, tn):
        a = _dot(xb, w_ref[:, c0:c0 + tn]) + b_ref[:, c0:c0 + tn]
        gate = _dot(xb, w_ref[:, D_MODEL + c0:D_MODEL + c0 + tn]) + b_ref[:, D_MODEL + c0:D_MODEL + c0 + tn]
        u_ref[:, c0:c0 + tn] = a * jax.nn.sigmoid(gate)


def _glu_proj(x, w_in, b_in, tm, tn):
    t = x.shape[0]
    body = functools.partial(_glu_body, tn=tn)
    return pl.pallas_call(
        body,
        grid=(t // tm,),
        in_specs=[pl.BlockSpec((tm, D_MODEL), lambda i: (i, 0)),
                  _resident((D_MODEL, 2 * D_MODEL)), _resident((1, 2 * D_MODEL))],
        out_specs=pl.BlockSpec((tm, D_MODEL), lambda i: (i, 0)),
        out_shape=jax.ShapeDtypeStruct((t, D_MODEL), F32),
        compiler_params=_cparams("parallel"),
        name="conv_glu",
    )(x, w_in, b_in.reshape(1, 2 * D_MODEL))


def _conv_out_body(u_ref, up_ref, un_ref, x_ref, dw_ref, dwb_ref, ng_ref, nb_ref, w_ref, bo_ref, g_ref, b_ref,
                   y_ref, buf, act, *, ts):
    i = pl.program_id(1)
    halo = CONV_HALO
    pad = CONV_WIDTH // 2
    for s in range(D_MODEL // CONV_LANES):
        cs = slice(s * CONV_LANES, (s + 1) * CONV_LANES)
        buf[s, 0:halo, :] = jnp.where(i > 0, up_ref[0, :, cs], 0.0)
        buf[s, halo:halo + ts, :] = u_ref[0, :, cs]
        buf[s, halo + ts:halo + ts + halo, :] = jnp.where(i < pl.num_programs(1) - 1, un_ref[0, :, cs], 0.0)
    dwb, ng, nb = dwb_ref[...], ng_ref[...], nb_ref[...]
    n_tiles = ts // CONV_ROWS

    def conv_tile(t):
        base = pl.multiple_of(t * CONV_ROWS, CONV_ROWS)
        n_groups = CONV_ROWS // SUBLANES
        cols = []
        for s in range(D_MODEL // CONV_LANES):
            lanes = slice(s * CONV_LANES, (s + 1) * CONV_LANES)
            accs = [jnp.zeros((SUBLANES, CONV_LANES), F32) for _ in range(n_groups)]
            for o in range(halo - pad, CONV_ROWS + halo + pad - SUBLANES + 1):
                data = buf[s, pl.ds(base + o, SUBLANES, stride=1), :]
                for g in range(n_groups):
                    j = o - (halo - pad) - SUBLANES * g
                    if 0 <= j < CONV_WIDTH:
                        accs[g] = accs[g] + dw_ref[j:j + 1, lanes] * data
            cols.append(jnp.concatenate(accs, axis=0))
        c = _layer_norm(jnp.concatenate(cols, axis=1) + dwb, ng, nb)
        act[t % 2] = (c * jax.nn.sigmoid(c)).astype(BF16)

    def out_tile(t):
        rows = pl.ds(pl.multiple_of(t * CONV_ROWS, CONV_ROWS), CONV_ROWS)
        mix = _dot(act[t % 2], w_ref[...]) + bo_ref[...]
        y_ref[0, rows, :] = _layer_norm(ALPHA * x_ref[0, rows, :] + mix, g_ref[...], b_ref[...])

    conv_tile(0)

    def pipelined(t, carry):
        conv_tile(t)
        out_tile(t - 1)
        return carry

    lax.fori_loop(1, n_tiles, pipelined, 0)
    out_tile(n_tiles - 1)


def _conv_out(u, x, dw, dw_b, n_g, n_b, w_out, b_out, ln_g, ln_b, ts):
    bsz, seq, d = x.shape
    hb = ts // CONV_HALO
    last_halo = seq // CONV_HALO - 1

    def vec():
        return _resident((1, d))

    body = functools.partial(_conv_out_body, ts=ts)
    return pl.pallas_call(
        body,
        grid=(bsz, seq // ts),
        in_specs=[pl.BlockSpec((1, ts, d), lambda b, i: (b, i, 0)),
                  pl.BlockSpec((1, CONV_HALO, d), lambda b, i: (b, jnp.maximum(i * hb - 1, 0), 0)),
                  pl.BlockSpec((1, CONV_HALO, d), lambda b, i: (b, jnp.minimum((i + 1) * hb, last_halo), 0)),
                  pl.BlockSpec((1, ts, d), lambda b, i: (b, i, 0)),
                  _resident((CONV_WIDTH, d)),
                  vec(), vec(), vec(),
                  _resident((d, d)),
                  vec(), vec(), vec()],
        out_specs=pl.BlockSpec((1, ts, d), lambda b, i: (b, i, 0)),
        out_shape=jax.ShapeDtypeStruct((bsz, seq, d), F32),
        scratch_shapes=[pltpu.VMEM((d // CONV_LANES, ts + 2 * CONV_HALO, CONV_LANES), F32),
                        pltpu.VMEM((2, CONV_ROWS, d), BF16)],
        compiler_params=_cparams("parallel", "arbitrary"),
        name="conv_out_ln",
    )(u, u, u, x, dw, dw_b.reshape(1, d), n_g.reshape(1, d), n_b.reshape(1, d), w_out,
      b_out.reshape(1, d), ln_g.reshape(1, d), ln_b.reshape(1, d))


def _tiles(tokens):
    tm = 1024 if tokens % 1024 == 0 else 256
    return tm


def _trunk(x, rel_bias, hgrn_lb, w_in_ab, hgrn_norm, w_out_ab, w_in_c, b_in_c, dw_c, dw_b_c, cnorm_g, cnorm_b,
           w_out_c, b_out_c, ln_mix_g, ln_mix_b, mlp_w1, mlp_w2, ln_ffn_g, ln_ffn_b):
    bsz, seq, d = x.shape
    t = bsz * seq
    tm = _tiles(t)
    x2 = x.reshape(t, d)
    for l in range(DEPTH):
        i = l // 2
        if l % 2 == 0:
            wa = w_in_ab[i][:, :5 * A_WIDTH].reshape(d, 5, A_HEADS, A_HEAD_DIM)
            wa = wa.transpose(0, 2, 1, 3).reshape(d, 5 * A_WIDTH).astype(BF16)
            wb = w_in_ab[i][:, 5 * A_WIDTH:].astype(BF16)
            chunk = min(HGRN_CHUNK, seq)
            hg, hb = _in_proj(x2, wa, wb, hgrn_lb, hgrn_norm[i], min(IN_PROJ_ROWS, seq), chunk, l)
            oa = _hgrn_mixer(hg.reshape(bsz, seq, HG_PLANES * A_WIDTH), chunk)
            hb3 = hb.reshape(bsz, seq, 3 * B_WIDTH)
            outs, lses = [], []
            for gi, (window, r) in enumerate(B_GROUPS):
                hs = slice(gi * B_HEADS_PER_GROUP, (gi + 1) * B_HEADS_PER_GROUP)
                o, lse = _dilated_group(hb3, rel_bias[:, hs], gi, window, r)
                outs.append(o.reshape(t, B_GROUP_WIDTH))
                lses.append(lse.reshape(t, B_GROUP_WIDTH))
            x2 = _ab_out(oa.reshape(t, A_WIDTH), outs, lses, x2, w_out_ab[i].astype(BF16),
                         ln_mix_g[l], ln_mix_b[l], tm)
        else:
            u = _glu_proj(x2, w_in_c[i].astype(BF16), b_in_c[i], min(GLU_ROWS, t), GLU_COLS)
            x2 = _conv_out(u.reshape(bsz, seq, d), x2.reshape(bsz, seq, d), dw_c[i], dw_b_c[i], cnorm_g[i],
                           cnorm_b[i], w_out_c[i].astype(BF16), b_out_c[i], ln_mix_g[l], ln_mix_b[l],
                           min(CONV_BLOCK, seq)).reshape(t, d)
        x2 = _mlp(x2, mlp_w1[l].astype(BF16), mlp_w2[l].astype(BF16), ln_ffn_g[l], ln_ffn_b[l],
                  min(MLP_ROWS, t), min(MLP_SUB_ROWS, t))
    return x2.reshape(bsz, seq, d)


def kernel(x_prompt, x_sample, rel_bias, hgrn_lb, w_in_ab, hgrn_norm, w_out_ab, w_in_c, b_in_c, dw_c, dw_b_c,
           cnorm_g, cnorm_b, w_out_c, b_out_c, ln_mix_g, ln_mix_b, mlp_w1, mlp_w2, ln_ffn_g, ln_ffn_b):
    params = (rel_bias, hgrn_lb, w_in_ab, hgrn_norm, w_out_ab, w_in_c, b_in_c, dw_c, dw_b_c, cnorm_g, cnorm_b,
              w_out_c, b_out_c, ln_mix_g, ln_mix_b, mlp_w1, mlp_w2, ln_ffn_g, ln_ffn_b)
    return (_trunk(x_prompt, *params), _trunk(x_sample, *params))
```

```python
import functools

import numpy as np
import jax
import jax.numpy as jnp
from jax import lax
from jax.experimental import pallas as pl
from jax.experimental.pallas import tpu as pltpu

F32 = jnp.float32
BF16 = jnp.bfloat16

D_MODEL = 1024
DEPTH = 2
A_HEADS = 4
A_HEAD_DIM = 128
A_WIDTH = A_HEADS * A_HEAD_DIM
B_GROUPS = ((128, 1), (512, 4), (2048, 16))
B_HEADS_PER_GROUP = 4
B_HEAD_DIM = 64
B_GROUP_WIDTH = B_HEADS_PER_GROUP * B_HEAD_DIM
B_WIDTH = B_GROUP_WIDTH * len(B_GROUPS)
CONV_WIDTH = 31
D_FF = 4 * D_MODEL
REL_BUCKETS = 32
REL_MAX_DIST = 1024
ALPHA = (2 * DEPTH) ** 0.25
LN_EPS = 1e-5
RMS_EPS = 1e-6
NEG_INF = -1e30

SUBLANES = 8
IN_PROJ_ROWS = 512
HGRN_CHUNK = 128
HGRN_UNROLL = 8
ATT_QBLOCK = 128
ATT_LANES = 128
ATT_UNROLL = 8
ATT_SPLIT_STRIDE = 4
MLP_ROWS = 1024
MLP_SUB_ROWS = 256
GLU_ROWS = 1024
GLU_COLS = 512
CONV_BLOCK = 1024
CONV_HALO = 16
CONV_ROWS = 128
CONV_LANES = 128
VMEM_LIMIT = 52 * 1024 * 1024


def _cparams(*sem):
    return pltpu.CompilerParams(dimension_semantics=sem, vmem_limit_bytes=VMEM_LIMIT)


def _resident(shape):
    return pl.BlockSpec(shape, lambda *_: (0,) * len(shape), pipeline_mode=pl.Buffered(1))


def _dot(a, b):
    return jnp.dot(a, b, preferred_element_type=F32)


def _dot_nt(a, b):
    return lax.dot_general(a, b, (((1,), (1,)), ((), ())), preferred_element_type=F32)


def _dot_tn(a, b):
    return lax.dot_general(a, b, (((0,), (0,)), ((), ())), preferred_element_type=F32)


def _layer_norm(y, g, b):
    mu = jnp.mean(y, -1, keepdims=True)
    yc = y - mu
    var = jnp.mean(jnp.square(yc), -1, keepdims=True)
    return yc * lax.rsqrt(var + LN_EPS) * g + b


HG_PLANES = 7


def _in_proj_body(x_ref, wa_ref, wb_ref, lbp_ref, gn_ref, hg_ref, hb_ref, ha, *, chunk, layer):
    tm = x_ref.shape[0]
    dk = A_HEAD_DIM
    xb = x_ref[...].astype(BF16)
    n_groups = chunk // SUBLANES
    grp = (n_groups, SUBLANES, dk)
    sub = lax.broadcasted_iota(jnp.int32, grp, 1)

    def lower_bound(direction, lanes):
        p = [lbp_ref[direction, l:l + 1, lanes] for l in range(DEPTH + 1)]
        m = functools.reduce(jnp.maximum, p)
        e = [jnp.exp(v - m) for v in p]
        return sum(e[:layer + 1]) / sum(e)

    def gates(z, lb_d):
        sig = jax.nn.sigmoid(z)
        log2f = jnp.log2(lb_d + (1.0 - lb_d) * sig)
        k = (1.0 - lb_d) * (1.0 - sig)
        return k, log2f

    def chunk_scan(x, reverse):
        x3 = x.reshape(grp)
        s = 1
        while s < SUBLANES:
            if reverse:
                x3 = x3 + jnp.where(sub < SUBLANES - s, pltpu.roll(x3, SUBLANES - s, 1), 0.0)
            else:
                x3 = x3 + jnp.where(sub >= s, pltpu.roll(x3, s, 1), 0.0)
            s *= 2
        edge = 0 if reverse else SUBLANES - 1
        order = range(n_groups - 1, -1, -1) if reverse else range(n_groups)
        run = jnp.zeros((1, 1, dk), F32)
        offsets = [None] * n_groups
        for g in order:
            offsets[g] = run
            run = run + x3[g:g + 1, edge:edge + 1, :]
        return (x3 + jnp.concatenate(offsets, axis=0)).reshape(chunk, dk)

    def plane(p, h):
        return slice(p * A_WIDTH + h * dk, p * A_WIDTH + (h + 1) * dk)

    head_cols = wa_ref.shape[1] // A_HEADS
    for hp in range(A_HEADS // 2):
        ha[hp] = _dot(xb, wa_ref[:, hp * 2 * head_cols:(hp + 1) * 2 * head_cols])
    hb_ref[...] = _dot(xb, wb_ref[...])
    for h in range(A_HEADS):
        lanes = slice(h * dk, (h + 1) * dk)
        lb_f, lb_b = lower_bound(0, lanes), lower_bound(1, lanes)
        gn = gn_ref[:, lanes]

        def part(p, rows):
            c0 = (h % 2) * head_cols + p * dk
            return ha[h // 2, rows, c0:c0 + dk]

        for r0 in range(0, tm, chunk):
            rows = slice(r0, r0 + chunk)
            q = part(0, rows)
            hg_ref[rows, plane(0, h)] = q * jax.nn.sigmoid(q)
            hg_ref[rows, plane(1, h)] = part(1, rows)
            k, lf = gates(part(2, rows), lb_f)
            hg_ref[rows, plane(2, h)] = k
            hg_ref[rows, plane(4, h)] = chunk_scan(lf, reverse=False)
            k, lg = gates(part(3, rows), lb_b)
            hg_ref[rows, plane(3, h)] = k
            hg_ref[rows, plane(5, h)] = chunk_scan(lg, reverse=True)
            g = part(4, rows)
            hg_ref[rows, plane(6, h)] = gn * (g * jax.nn.sigmoid(g))


def _in_proj(x, wa, wb, hgrn_lb, g_norm, tm, chunk, layer):
    m, k = x.shape
    na, nb = wa.shape[1], wb.shape[1]
    body = functools.partial(_in_proj_body, chunk=chunk, layer=layer)
    return pl.pallas_call(
        body,
        grid=(m // tm,),
        in_specs=[pl.BlockSpec((tm, k), lambda i: (i, 0)),
                  _resident((k, na)), _resident((k, nb)), _resident(hgrn_lb.shape), _resident((1, A_WIDTH))],
        out_specs=[pl.BlockSpec((tm, HG_PLANES * A_WIDTH), lambda i: (i, 0)),
                   pl.BlockSpec((tm, nb), lambda i: (i, 0))],
        out_shape=[jax.ShapeDtypeStruct((m, HG_PLANES * A_WIDTH), F32), jax.ShapeDtypeStruct((m, nb), F32)],
        scratch_shapes=[pltpu.VMEM((A_HEADS // 2, tm, 2 * na // A_HEADS), F32)],
        compiler_params=_cparams("parallel"),
        name="in_proj",
    )(x, wa, wb, hgrn_lb, g_norm.reshape(1, A_WIDTH))


def _hgrn_body(qs, v_ref, kf, kb, bf, cb, gs, o_ref, of, ob, sf, sb, level, *, seq, chunk):
    C = chunk
    n_chunks = seq // C
    dk = A_HEAD_DIM
    grp = (C // SUBLANES, SUBLANES, dk)

    sf[...] = jnp.zeros_like(sf)
    sb[...] = jnp.zeros_like(sb)
    sub = lax.broadcasted_iota(jnp.int32, grp, 1)
    level[...] = 31 - lax.clz(lax.broadcasted_iota(jnp.int32, (C, C), 0) ^ lax.broadcasted_iota(jnp.int32, (C, C), 1))

    def mid_row(x3, w):
        if w == 1:
            return jnp.where((sub & 1) == 1, x3, pltpu.roll(x3, SUBLANES - 1, 1))
        if w == 2:
            return jnp.where(sub < 4, jnp.broadcast_to(x3[:, 2:3, :], grp), jnp.broadcast_to(x3[:, 6:7, :], grp))
        return jnp.broadcast_to(x3[:, 4:5, :], grp)

    def intra(q, kfv, kbv, bfv, cbv, v):
        lv = level[...]
        scores = jnp.zeros((C, C), F32)
        q3 = q.reshape(grp)
        kf3, kb3, bf3, cb3 = kfv.reshape(grp), kbv.reshape(grp), bfv.reshape(grp), cbv.reshape(grp)
        w = 1
        while w < C:
            if w < SUBLANES:
                second = (sub & w) != 0
                d_f = bf3 - mid_row(bf3, w)
                d_b = cb3 - mid_row(cb3, w)
                qt = (q3 * jnp.exp2(jnp.where(second, d_f, d_b))).reshape(C, dk)
                kt = (jnp.where(second, kb3, kf3) * jnp.exp2(-jnp.where(second, d_b, d_f))).reshape(C, dk)
            else:
                qparts, kparts = [], []
                for r0 in range(0, C, 2 * w):
                    first, second = slice(r0, r0 + w), slice(r0 + w, r0 + 2 * w)
                    bf_r, cb_r = bfv[r0 + w:r0 + w + 1, :], cbv[r0 + w:r0 + w + 1, :]
                    qparts += [q[first] * jnp.exp2(cbv[first] - cb_r), q[second] * jnp.exp2(bfv[second] - bf_r)]
                    kparts += [kfv[first] * jnp.exp2(bf_r - bfv[first]), kbv[second] * jnp.exp2(cb_r - cbv[second])]
                qt, kt = jnp.concatenate(qparts, 0), jnp.concatenate(kparts, 0)
            p = _dot_nt(qt.astype(BF16), kt.astype(BF16))
            scores = jnp.where(lv == w.bit_length() - 1, p, scores)
            w *= 2
        same = jnp.sum(q * (kfv + kbv), axis=-1, keepdims=True) * v
        return same + _dot(scores.astype(BF16), v.astype(BF16))

    def finish(rows, o):
        o = o * lax.rsqrt(jnp.mean(o * o, -1, keepdims=True) + RMS_EPS)
        o_ref[rows, :] = (o * gs[rows, :]).astype(o_ref.dtype)

    def step(n, carry, *, other_half_done):
        rows = pl.ds(pl.multiple_of(n * C, C), C)
        q, kfv, kbv, bfv, cbv, v = qs[rows, :], kf[rows, :], kb[rows, :], bf[rows, :], cb[rows, :], v_ref[rows, :]
        o = intra(q, kfv, kbv, bfv, cbv, v)
        st = sf[...]
        o = o + _dot_nt((q * jnp.exp2(bfv)).astype(BF16), st.astype(BF16))
        if other_half_done:
            finish(rows, o + ob[rows, :])
        else:
            of[rows, :] = o
        b_last = bfv[C - 1:C, :]
        kt = (kfv * jnp.exp2(b_last - bfv)).astype(BF16)
        sf[...] = jnp.exp2(b_last) * st + _dot_tn(v.astype(BF16), kt)

        rows = pl.ds(pl.multiple_of((n_chunks - 1 - n) * C, C), C)
        q, kbv, cbv, v = qs[rows, :], kb[rows, :], cb[rows, :], v_ref[rows, :]
        st = sb[...]
        o = _dot_nt((q * jnp.exp2(cbv)).astype(BF16), st.astype(BF16))
        if other_half_done:
            finish(rows, o + of[rows, :])
        else:
            ob[rows, :] = o
        c_first = cbv[0:1, :]
        kt = (kbv * jnp.exp2(c_first - cbv)).astype(BF16)
        sb[...] = jnp.exp2(c_first) * st + _dot_tn(v.astype(BF16), kt)
        return carry

    if n_chunks % 2 == 0:
        half = n_chunks // 2
        unroll = HGRN_UNROLL if half % HGRN_UNROLL == 0 else 1
        lax.fori_loop(0, half, functools.partial(step, other_half_done=False), 0, unroll=unroll)
        lax.fori_loop(half, n_chunks, functools.partial(step, other_half_done=True), 0, unroll=unroll)
    else:
        lax.fori_loop(0, n_chunks, functools.partial(step, other_half_done=False), 0)
        for n in range(n_chunks):
            rows = slice(n * C, (n + 1) * C)
            finish(rows, of[rows, :] + ob[rows, :])


def _hgrn_mixer(hg, chunk):
    bsz, seq, _ = hg.shape
    dk = A_HEAD_DIM

    def col(part):
        return pl.BlockSpec((None, seq, dk), lambda b, h, part=part: (b, 0, part * A_HEADS + h))

    body = functools.partial(_hgrn_body, seq=seq, chunk=chunk)
    return pl.pallas_call(
        body,
        grid=(bsz, A_HEADS),
        in_specs=[col(p) for p in range(HG_PLANES)],
        out_specs=pl.BlockSpec((None, seq, dk), lambda b, h: (b, 0, h)),
        out_shape=jax.ShapeDtypeStruct((bsz, seq, A_WIDTH), BF16),
        scratch_shapes=[pltpu.VMEM((seq, dk), F32)] * 2 + [pltpu.VMEM((dk, dk), F32)] * 2
        + [pltpu.VMEM((chunk, chunk), jnp.int32)],
        compiler_params=_cparams("parallel", "arbitrary"),
        name="hgrn2",
    )(*([hg] * HG_PLANES))


def _t5_buckets(rel):
    half = REL_BUCKETS // 2
    max_exact = half // 2
    n = np.abs(rel)
    large = max_exact + (np.log(np.maximum(n, 1) / max_exact) / np.log(REL_MAX_DIST / max_exact)
                         * (half - max_exact)).astype(np.int32)
    large = np.minimum(large, half - 1)
    return (np.where(rel > 0, half, 0) + np.where(n < max_exact, n, large)).astype(np.int32)


def _att_geometry(sub_len, half):
    lq = min(ATT_QBLOCK, sub_len)
    win = min(lq + 2 * half, sub_len)
    return lq, win, sub_len // lq


def _att_bias(rel_bias_g, sub_len, dilation, half):
    lq, win, _ = _att_geometry(sub_len, half)
    p = win + lq - 1
    tables = []
    for delta in (0, -half, lq - win):
        rel = np.arange(p + 1) - (lq - 1) + delta
        valid = (np.abs(rel) <= half) & (np.arange(p + 1) < p)
        u = jnp.where(valid[:, None], rel_bias_g[_t5_buckets(rel * dilation)], NEG_INF).T
        flat = jnp.tile(u, (1, lq))[:, :lq * p]
        tables.append(flat.reshape(-1, lq, p)[:, :, lq - 1:lq - 1 + win])
    return jnp.stack(tables).astype(F32)


def _att_body(q_ref, k_ref, v_ref, bias_ref, o_ref, lse_ref, qd, kd, vd, oacc, *stage, seq, r, lq, win, half):
    sub_len = seq // r
    nq = sub_len // lq
    n_tasks = r * nq
    unroll = min(ATT_UNROLL, n_tasks)
    lanes = q_ref.shape[-1]
    heads = lanes // B_HEAD_DIM
    lane = lax.broadcasted_iota(jnp.int32, (lq, lanes), 1)
    head_lanes = [(lane >= h * B_HEAD_DIM) & (lane < (h + 1) * B_HEAD_DIM) for h in range(heads)]

    if stage:
        r1 = ATT_SPLIT_STRIDE
        len1 = seq // r1
        for src, dst in zip((q_ref, k_ref, v_ref), stage):
            for c1 in range(r1):
                dst[c1 * len1:(c1 + 1) * len1, :] = src[pl.ds(c1, len1, stride=r1), :]
        sources, r2 = stage, r // r1
    else:
        sources, r1, len1, r2 = (q_ref, k_ref, v_ref), 1, seq, r

    def split(c, carry):
        members = pl.ds((c % r1) * len1 + c // r1, sub_len, stride=r2)
        dst = pl.ds(pl.multiple_of(c * sub_len, sub_len), sub_len)
        qd[dst, :] = (sources[0][members, :] * (B_HEAD_DIM ** -0.5)).astype(BF16)
        kd[dst, :] = sources[1][members, :].astype(BF16)
        vd[dst, :] = sources[2][members, :].astype(BF16)
        return carry

    lax.fori_loop(0, r, split, 0)

    def task(tk):
        c = tk // nq
        iq = tk % nq
        q0 = iq * lq
        ws = jnp.clip(q0 - half, 0, sub_len - win)
        case = jnp.where(iq == 0, 0, jnp.where(iq == nq - 1, 2, 1))
        base = c * sub_len
        q = qd[pl.ds(pl.multiple_of(base + q0, lq), lq), :]
        kwin = kd[pl.ds(pl.multiple_of(base + ws, half), win), :]
        vwin = vd[pl.ds(pl.multiple_of(base + ws, half), win), :]
        qm = jnp.concatenate([jnp.where(hl, q, jnp.zeros_like(q)) for hl in head_lanes], axis=0)
        bias = jnp.concatenate([bias_ref[case, h] for h in range(heads)], axis=0)
        s = _dot_nt(qm, kwin) + bias
        m = jnp.max(s, -1, keepdims=True)
        p = jnp.exp(s - m)
        den = jnp.sum(p, -1, keepdims=True)
        o2 = _dot(p.astype(BF16), vwin) / den
        lse2 = m + jnp.log(den)
        out = o2[0:lq]
        lse = jnp.broadcast_to(lse2[0:lq], (lq, lanes))
        for h in range(1, heads):
            out = jnp.where(head_lanes[h], o2[h * lq:(h + 1) * lq], out)
            lse = jnp.where(head_lanes[h], lse2[h * lq:(h + 1) * lq], lse)
        dst = pl.ds(c + q0 * r, lq, stride=r)
        oacc[dst, :] = out
        lse_ref[dst, :] = lse

    def group(i, carry):
        for u in range(unroll):
            task(i * unroll + u)
        return carry

    lax.fori_loop(0, n_tasks // unroll, group, 0)
    o_ref[...] = oacc[...].astype(o_ref.dtype)


def _dilated_group(hb, rel_bias_g, gi, window, dilation):
    bsz, seq, _ = hb.shape
    r = dilation
    sub_len = seq // r
    half = window // (2 * r)
    lq, win, nq = _att_geometry(sub_len, half)
    lanes = ATT_LANES
    pairs = B_GROUP_WIDTH // lanes
    heads = lanes // B_HEAD_DIM
    bias = _att_bias(rel_bias_g, sub_len, r, half)

    def part(p):
        return pl.BlockSpec((None, seq, lanes), lambda b, hp, p=p: (b, 0, (p * len(B_GROUPS) + gi) * pairs + hp))

    def out_spec():
        return pl.BlockSpec((None, seq, lanes), lambda b, hp: (b, 0, hp))

    body = functools.partial(_att_body, seq=seq, r=r, lq=lq, win=win, half=half)
    return pl.pallas_call(
        body,
        grid=(bsz, pairs),
        in_specs=[part(0), part(1), part(2),
                  pl.BlockSpec((3, heads, lq, win), lambda b, hp: (0, hp, 0, 0))],
        out_specs=[out_spec(), out_spec()],
        out_shape=[jax.ShapeDtypeStruct((bsz, seq, B_GROUP_WIDTH), BF16),
                   jax.ShapeDtypeStruct((bsz, seq, B_GROUP_WIDTH), F32)],
        scratch_shapes=[pltpu.VMEM((seq, lanes), BF16)] * 3 + [pltpu.VMEM((seq, lanes), F32)]
        + ([pltpu.VMEM((seq, lanes), F32)] * 3 if r > ATT_SPLIT_STRIDE else []),
        compiler_params=_cparams("parallel", "arbitrary"),
        name=f"dilated_attn_r{r}",
    )(hb, hb, hb, bias)


def _ab_out_body(oa_ref, o1_ref, o2_ref, o3_ref, l1_ref, l2_ref, l3_ref, x_ref, w_ref, g_ref, b_ref, y_ref):
    l1, l2, l3 = l1_ref[...], l2_ref[...], l3_ref[...]
    m = jnp.maximum(jnp.maximum(l1, l2), l3)
    e1, e2, e3 = jnp.exp(l1 - m), jnp.exp(l2 - m), jnp.exp(l3 - m)
    tot = e1 + e2 + e3
    mix = _dot(oa_ref[...], w_ref[0:A_WIDTH, :])
    for gi, (o_ref, e) in enumerate(((o1_ref, e1), (o2_ref, e2), (o3_ref, e3))):
        og = (o_ref[...].astype(F32) * (e / tot)).astype(BF16)
        lo = A_WIDTH + gi * B_GROUP_WIDTH
        mix = mix + _dot(og, w_ref[lo:lo + B_GROUP_WIDTH, :])
    y_ref[...] = _layer_norm(ALPHA * x_ref[...] + mix, g_ref[...], b_ref[...])


def _ab_out(oa, outs, lses, x, w_out, ln_g, ln_b, tm):
    t = x.shape[0]
    gw = B_GROUP_WIDTH

    def rowblock(width):
        return pl.BlockSpec((tm, width), lambda i: (i, 0))

    return pl.pallas_call(
        _ab_out_body,
        grid=(t // tm,),
        in_specs=[rowblock(A_WIDTH)] + [rowblock(gw)] * 6 + [rowblock(D_MODEL), _resident(w_out.shape),
                                                           _resident((1, D_MODEL)), _resident((1, D_MODEL))],
        out_specs=rowblock(D_MODEL),
        out_shape=jax.ShapeDtypeStruct((t, D_MODEL), F32),
        compiler_params=_cparams("parallel"),
        name="ab_out_ln",
    )(oa, *outs, *lses, x, w_out, ln_g.reshape(1, D_MODEL), ln_b.reshape(1, D_MODEL))


def _mlp_body(x_ref, w1_ref, w2_ref, g_ref, b_ref, y_ref, *, sub):
    for r0 in range(0, x_ref.shape[0], sub):
        rows = slice(r0, r0 + sub)
        x = x_ref[rows, :]
        h = jnp.square(jnp.maximum(_dot(x.astype(BF16), w1_ref[...]), 0.0)).astype(BF16)
        y_ref[rows, :] = _layer_norm(ALPHA * x + _dot(h, w2_ref[...]), g_ref[...], b_ref[...])


def _mlp(x, w1, w2, ln_g, ln_b, tm, sub):
    t = x.shape[0]
    body = functools.partial(_mlp_body, sub=sub)
    return pl.pallas_call(
        body,
        grid=(t // tm,),
        in_specs=[pl.BlockSpec((tm, D_MODEL), lambda i: (i, 0)),
                  _resident(w1.shape), _resident(w2.shape),
                  _resident((1, D_MODEL)), _resident((1, D_MODEL))],
        out_specs=pl.BlockSpec((tm, D_MODEL), lambda i: (i, 0)),
        out_shape=jax.ShapeDtypeStruct((t, D_MODEL), F32),
        compiler_params=_cparams("parallel"),
        name="mlp_ln",
    )(x, w1, w2, ln_g.reshape(1, D_MODEL), ln_b.reshape(1, D_MODEL))


def _glu_body(x_ref, w_ref, b_ref, u_ref, *, tn):
    xb = x_ref[...].astype(BF16)
    for c0 in range(0, D_MODEL, tn):
        a = _dot(xb, w_ref[:, c0:c0 + tn]) + b_ref[:, c0:c0 + tn]
        gate = _dot(xb, w_ref[:, D_MODEL + c0:D_MODEL + c0 + tn]) + b_ref[:, D_MODEL + c0:D_MODEL + c0 + tn]
        u_ref[:, c0:c0 + tn] = a * jax.nn.sigmoid(gate)


def _glu_proj(x, w_in, b_in, tm, tn):
    t = x.shape[0]
    body = functools.partial(_glu_body, tn=tn)
    return pl.pallas_call(
        body,
        grid=(t // tm,),
        in_specs=[pl.BlockSpec((tm, D_MODEL), lambda i: (i, 0)),
                  _resident((D_MODEL, 2 * D_MODEL)), _resident((1, 2 * D_MODEL))],
        out_specs=pl.BlockSpec((tm, D_MODEL), lambda i: (i, 0)),
        out_shape=jax.ShapeDtypeStruct((t, D_MODEL), F32),
        compiler_params=_cparams("parallel"),
        name="conv_glu",
    )(x, w_in, b_in.reshape(1, 2 * D_MODEL))


def _conv_out_body(u_ref, up_ref, un_ref, x_ref, dw_ref, dwb_ref, ng_ref, nb_ref, w_ref, bo_ref, g_ref, b_ref,
                   y_ref, buf, act, *, ts):
    i = pl.program_id(1)
    halo = CONV_HALO
    pad = CONV_WIDTH // 2
    for s in range(D_MODEL // CONV_LANES):
        cs = slice(s * CONV_LANES, (s + 1) * CONV_LANES)
        buf[s, 0:halo, :] = jnp.where(i > 0, up_ref[0, :, cs], 0.0)
        buf[s, halo:halo + ts, :] = u_ref[0, :, cs]
        buf[s, halo + ts:halo + ts + halo, :] = jnp.where(i < pl.num_programs(1) - 1, un_ref[0, :, cs], 0.0)
    dwb, ng, nb = dwb_ref[...], ng_ref[...], nb_ref[...]
    n_tiles = ts // CONV_ROWS

    def conv_tile(t):
        base = pl.multiple_of(t * CONV_ROWS, CONV_ROWS)
        n_groups = CONV_ROWS // SUBLANES
        cols = []
        for s in range(D_MODEL // CONV_LANES):
            lanes = slice(s * CONV_LANES, (s + 1) * CONV_LANES)
            accs = [jnp.zeros((SUBLANES, CONV_LANES), F32) for _ in range(n_groups)]
            for o in range(halo - pad, CONV_ROWS + halo + pad - SUBLANES + 1):
                data = buf[s, pl.ds(base + o, SUBLANES, stride=1), :]
                for g in range(n_groups):
                    j = o - (halo - pad) - SUBLANES * g
                    if 0 <= j < CONV_WIDTH:
                        accs[g] = accs[g] + dw_ref[j:j + 1, lanes] * data
            cols.append(jnp.concatenate(accs, axis=0))
        c = _layer_norm(jnp.concatenate(cols, axis=1) + dwb, ng, nb)
        act[t % 2] = (c * jax.nn.sigmoid(c)).astype(BF16)

    def out_tile(t):
        rows = pl.ds(pl.multiple_of(t * CONV_ROWS, CONV_ROWS), CONV_ROWS)
        mix = _dot(act[t % 2], w_ref[...]) + bo_ref[...]
        y_ref[0, rows, :] = _layer_norm(ALPHA * x_ref[0, rows, :] + mix, g_ref[...], b_ref[...])

    conv_tile(0)

    def pipelined(t, carry):
        conv_tile(t)
        out_tile(t - 1)
        return carry

    lax.fori_loop(1, n_tiles, pipelined, 0)
    out_tile(n_tiles - 1)


def _conv_out(u, x, dw, dw_b, n_g, n_b, w_out, b_out, ln_g, ln_b, ts):
    bsz, seq, d = x.shape
    hb = ts // CONV_HALO
    last_halo = seq // CONV_HALO - 1

    def vec():
        return _resident((1, d))

    body = functools.partial(_conv_out_body, ts=ts)
    return pl.pallas_call(
        body,
        grid=(bsz, seq // ts),
        in_specs=[pl.BlockSpec((1, ts, d), lambda b, i: (b, i, 0)),
                  pl.BlockSpec((1, CONV_HALO, d), lambda b, i: (b, jnp.maximum(i * hb - 1, 0), 0)),
                  pl.BlockSpec((1, CONV_HALO, d), lambda b, i: (b, jnp.minimum((i + 1) * hb, last_halo), 0)),
                  pl.BlockSpec((1, ts, d), lambda b, i: (b, i, 0)),
                  _resident((CONV_WIDTH, d)),
                  vec(), vec(), vec(),
                  _resident((d, d)),
                  vec(), vec(), vec()],
        out_specs=pl.BlockSpec((1, ts, d), lambda b, i: (b, i, 0)),
        out_shape=jax.ShapeDtypeStruct((bsz, seq, d), F32),
        scratch_shapes=[pltpu.VMEM((d // CONV_LANES, ts + 2 * CONV_HALO, CONV_LANES), F32),
                        pltpu.VMEM((2, CONV_ROWS, d), BF16)],
        compiler_params=_cparams("parallel", "arbitrary"),
        name="conv_out_ln",
    )(u, u, u, x, dw, dw_b.reshape(1, d), n_g.reshape(1, d), n_b.reshape(1, d), w_out,
      b_out.reshape(1, d), ln_g.reshape(1, d), ln_b.reshape(1, d))


def _tiles(tokens):
    tm = 1024 if tokens % 1024 == 0 else 256
    return tm


def _trunk(x, rel_bias, hgrn_lb, w_in_ab, hgrn_norm, w_out_ab, w_in_c, b_in_c, dw_c, dw_b_c, cnorm_g, cnorm_b,
           w_out_c, b_out_c, ln_mix_g, ln_mix_b, mlp_w1, mlp_w2, ln_ffn_g, ln_ffn_b):
    bsz, seq, d = x.shape
    t = bsz * seq
    tm = _tiles(t)
    x2 = x.reshape(t, d)
    for l in range(DEPTH):
        i = l // 2
        if l % 2 == 0:
            wa = w_in_ab[i][:, :5 * A_WIDTH].reshape(d, 5, A_HEADS, A_HEAD_DIM)
            wa = wa.transpose(0, 2, 1, 3).reshape(d, 5 * A_WIDTH).astype(BF16)
            wb = w_in_ab[i][:, 5 * A_WIDTH:].astype(BF16)
            chunk = min(HGRN_CHUNK, seq)
            hg, hb = _in_proj(x2, wa, wb, hgrn_lb, hgrn_norm[i], min(IN_PROJ_ROWS, seq), chunk, l)
            oa = _hgrn_mixer(hg.reshape(bsz, seq, HG_PLANES * A_WIDTH), chunk)
            hb3 = hb.reshape(bsz, seq, 3 * B_WIDTH)
            outs, lses = [], []
            for gi, (window, r) in enumerate(B_GROUPS):
                hs = slice(gi * B_HEADS_PER_GROUP, (gi + 1) * B_HEADS_PER_GROUP)
                o, lse = _dilated_group(hb3, rel_bias[:, hs], gi, window, r)
                outs.append(o.reshape(t, B_GROUP_WIDTH))
                lses.append(lse.reshape(t, B_GROUP_WIDTH))
            x2 = _ab_out(oa.reshape(t, A_WIDTH), outs, lses, x2, w_out_ab[i].astype(BF16),
                         ln_mix_g[l], ln_mix_b[l], tm)
        else:
            u = _glu_proj(x2, w_in_c[i].astype(BF16), b_in_c[i], min(GLU_ROWS, t), GLU_COLS)
            x2 = _conv_out(u.reshape(bsz, seq, d), x2.reshape(bsz, seq, d), dw_c[i], dw_b_c[i], cnorm_g[i],
                           cnorm_b[i], w_out_c[i].astype(BF16), b_out_c[i], ln_mix_g[l], ln_mix_b[l],
                           min(CONV_BLOCK, seq)).reshape(t, d)
        x2 = _mlp(x2, mlp_w1[l].astype(BF16), mlp_w2[l].astype(BF16), ln_ffn_g[l], ln_ffn_b[l],
                  min(MLP_ROWS, t), min(MLP_SUB_ROWS, t))
    return x2.reshape(bsz, seq, d)


def kernel(x_prompt, x_sample, rel_bias, hgrn_lb, w_in_ab, hgrn_norm, w_out_ab, w_in_c, b_in_c, dw_c, dw_b_c,
           cnorm_g, cnorm_b, w_out_c, b_out_c, ln_mix_g, ln_mix_b, mlp_w1, mlp_w2, ln_ffn_g, ln_ffn_b):
    params = (rel_bias, hgrn_lb, w_in_ab, hgrn_norm, w_out_ab, w_in_c, b_in_c, dw_c, dw_b_c, cnorm_g, cnorm_b,
              w_out_c, b_out_c, ln_mix_g, ln_mix_b, mlp_w1, mlp_w2, ln_ffn_g, ln_ffn_b)
    return (_trunk(x_prompt, *params), _trunk(x_sample, *params))
```

```python
import functools

import numpy as np
import jax
import jax.numpy as jnp
from jax import lax
from jax.experimental import pallas as pl
from jax.experimental.pallas import tpu as pltpu

F32 = jnp.float32
BF16 = jnp.bfloat16

D_MODEL = 1024
DEPTH = 2
A_HEADS = 4
A_HEAD_DIM = 128
A_WIDTH = A_HEADS * A_HEAD_DIM
B_GROUPS = ((128, 1), (512, 4), (2048, 16))
B_HEADS_PER_GROUP = 4
B_HEAD_DIM = 64
B_GROUP_WIDTH = B_HEADS_PER_GROUP * B_HEAD_DIM
B_WIDTH = B_GROUP_WIDTH * len(B_GROUPS)
CONV_WIDTH = 31
D_FF = 4 * D_MODEL
REL_BUCKETS = 32
REL_MAX_DIST = 1024
ALPHA = (2 * DEPTH) ** 0.25
LN_EPS = 1e-5
RMS_EPS = 1e-6
NEG_INF = -1e30

SUBLANES = 8
IN_PROJ_ROWS = 512
HGRN_CHUNK = 128
HGRN_UNROLL = 8
ATT_QBLOCK = 128
ATT_LANES = 128
ATT_UNROLL = 8
ATT_SPLIT_STRIDE = 4
MLP_ROWS = 1024
AB_MLP_ROWS = 512
MLP_SUB_ROWS = 256
GLU_ROWS = 1024
GLU_COLS = 512
CONV_BLOCK = 1024
CONV_HALO = 16
CONV_ROWS = 128
CONV_LANES = 128
VMEM_LIMIT = 52 * 1024 * 1024


def _cparams(*sem):
    return pltpu.CompilerParams(dimension_semantics=sem, vmem_limit_bytes=VMEM_LIMIT)


def _resident(shape):
    return pl.BlockSpec(shape, lambda *_: (0,) * len(shape), pipeline_mode=pl.Buffered(1))


def _dot(a, b):
    return jnp.dot(a, b, preferred_element_type=F32)


def _dot_nt(a, b):
    return lax.dot_general(a, b, (((1,), (1,)), ((), ())), preferred_element_type=F32)


def _dot_tn(a, b):
    return lax.dot_general(a, b, (((0,), (0,)), ((), ())), preferred_element_type=F32)


def _layer_norm(y, g, b):
    mu = jnp.mean(y, -1, keepdims=True)
    yc = y - mu
    var = jnp.mean(jnp.square(yc), -1, keepdims=True)
    return yc * lax.rsqrt(var + LN_EPS) * g + b


HG_PLANES = 7


def _in_proj_body(x_ref, wa_ref, wb_ref, lbp_ref, gn_ref, hg_ref, hb_ref, ha, *, chunk, layer):
    tm = x_ref.shape[0]
    dk = A_HEAD_DIM
    xb = x_ref[...].astype(BF16)
    n_groups = chunk // SUBLANES
    grp = (n_groups, SUBLANES, dk)
    sub = lax.broadcasted_iota(jnp.int32, grp, 1)

    def lower_bound(direction, lanes):
        p = [lbp_ref[direction, l:l + 1, lanes] for l in range(DEPTH + 1)]
        m = functools.reduce(jnp.maximum, p)
        e = [jnp.exp(v - m) for v in p]
        return sum(e[:layer + 1]) / sum(e)

    def gates(z, lb_d):
        sig = jax.nn.sigmoid(z)
        log2f = jnp.log2(lb_d + (1.0 - lb_d) * sig)
        k = (1.0 - lb_d) * (1.0 - sig)
        return k, log2f

    def chunk_scan(x, reverse):
        x3 = x.reshape(grp)
        s = 1
        while s < SUBLANES:
            if reverse:
                x3 = x3 + jnp.where(sub < SUBLANES - s, pltpu.roll(x3, SUBLANES - s, 1), 0.0)
            else:
                x3 = x3 + jnp.where(sub >= s, pltpu.roll(x3, s, 1), 0.0)
            s *= 2
        edge = 0 if reverse else SUBLANES - 1
        order = range(n_groups - 1, -1, -1) if reverse else range(n_groups)
        run = jnp.zeros((1, 1, dk), F32)
        offsets = [None] * n_groups
        for g in order:
            offsets[g] = run
            run = run + x3[g:g + 1, edge:edge + 1, :]
        return (x3 + jnp.concatenate(offsets, axis=0)).reshape(chunk, dk)

    def plane(p, h):
        return slice(p * A_WIDTH + h * dk, p * A_WIDTH + (h + 1) * dk)

    head_cols = wa_ref.shape[1] // A_HEADS
    for hp in range(A_HEADS // 2):
        ha[hp] = _dot(xb, wa_ref[:, hp * 2 * head_cols:(hp + 1) * 2 * head_cols])
    hb_ref[...] = _dot(xb, wb_ref[...])
    for h in range(A_HEADS):
        lanes = slice(h * dk, (h + 1) * dk)
        lb_f, lb_b = lower_bound(0, lanes), lower_bound(1, lanes)
        gn = gn_ref[:, lanes]

        def part(p, rows):
            c0 = (h % 2) * head_cols + p * dk
            return ha[h // 2, rows, c0:c0 + dk]

        for r0 in range(0, tm, chunk):
            rows = slice(r0, r0 + chunk)
            q = part(0, rows)
            hg_ref[rows, plane(0, h)] = q * jax.nn.sigmoid(q)
            hg_ref[rows, plane(1, h)] = part(1, rows)
            k, lf = gates(part(2, rows), lb_f)
            hg_ref[rows, plane(2, h)] = k
            hg_ref[rows, plane(4, h)] = chunk_scan(lf, reverse=False)
            k, lg = gates(part(3, rows), lb_b)
            hg_ref[rows, plane(3, h)] = k
            hg_ref[rows, plane(5, h)] = chunk_scan(lg, reverse=True)
            g = part(4, rows)
            hg_ref[rows, plane(6, h)] = gn * (g * jax.nn.sigmoid(g))


def _in_proj(x, wa, wb, hgrn_lb, g_norm, tm, chunk, layer):
    m, k = x.shape
    na, nb = wa.shape[1], wb.shape[1]
    body = functools.partial(_in_proj_body, chunk=chunk, layer=layer)
    return pl.pallas_call(
        body,
        grid=(m // tm,),
        in_specs=[pl.BlockSpec((tm, k), lambda i: (i, 0)),
                  _resident((k, na)), _resident((k, nb)), _resident(hgrn_lb.shape), _resident((1, A_WIDTH))],
        out_specs=[pl.BlockSpec((tm, HG_PLANES * A_WIDTH), lambda i: (i, 0)),
                   pl.BlockSpec((tm, nb), lambda i: (i, 0))],
        out_shape=[jax.ShapeDtypeStruct((m, HG_PLANES * A_WIDTH), F32), jax.ShapeDtypeStruct((m, nb), F32)],
        scratch_shapes=[pltpu.VMEM((A_HEADS // 2, tm, 2 * na // A_HEADS), F32)],
        compiler_params=_cparams("parallel"),
        name="in_proj",
    )(x, wa, wb, hgrn_lb, g_norm.reshape(1, A_WIDTH))


def _hgrn_body(qs, v_ref, kf, kb, bf, cb, gs, o_ref, of, ob, sf, sb, level, *, seq, chunk):
    C = chunk
    n_chunks = seq // C
    dk = A_HEAD_DIM
    grp = (C // SUBLANES, SUBLANES, dk)

    sf[...] = jnp.zeros_like(sf)
    sb[...] = jnp.zeros_like(sb)
    sub = lax.broadcasted_iota(jnp.int32, grp, 1)
    level[...] = 31 - lax.clz(lax.broadcasted_iota(jnp.int32, (C, C), 0) ^ lax.broadcasted_iota(jnp.int32, (C, C), 1))

    def mid_row(x3, w):
        if w == 1:
            return jnp.where((sub & 1) == 1, x3, pltpu.roll(x3, SUBLANES - 1, 1))
        if w == 2:
            return jnp.where(sub < 4, jnp.broadcast_to(x3[:, 2:3, :], grp), jnp.broadcast_to(x3[:, 6:7, :], grp))
        return jnp.broadcast_to(x3[:, 4:5, :], grp)

    def intra(q, kfv, kbv, bfv, cbv, v):
        lv = level[...]
        scores = jnp.zeros((C, C), F32)
        q3 = q.reshape(grp)
        kf3, kb3, bf3, cb3 = kfv.reshape(grp), kbv.reshape(grp), bfv.reshape(grp), cbv.reshape(grp)
        w = 1
        while w < C:
            if w < SUBLANES:
                second = (sub & w) != 0
                d_f = bf3 - mid_row(bf3, w)
                d_b = cb3 - mid_row(cb3, w)
                qt = (q3 * jnp.exp2(jnp.where(second, d_f, d_b))).reshape(C, dk)
                kt = (jnp.where(second, kb3, kf3) * jnp.exp2(-jnp.where(second, d_b, d_f))).reshape(C, dk)
            else:
                qparts, kparts = [], []
                for r0 in range(0, C, 2 * w):
                    first, second = slice(r0, r0 + w), slice(r0 + w, r0 + 2 * w)
                    bf_r, cb_r = bfv[r0 + w:r0 + w + 1, :], cbv[r0 + w:r0 + w + 1, :]
                    qparts += [q[first] * jnp.exp2(cbv[first] - cb_r), q[second] * jnp.exp2(bfv[second] - bf_r)]
                    kparts += [kfv[first] * jnp.exp2(bf_r - bfv[first]), kbv[second] * jnp.exp2(cb_r - cbv[second])]
                qt, kt = jnp.concatenate(qparts, 0), jnp.concatenate(kparts, 0)
            p = _dot_nt(qt.astype(BF16), kt.astype(BF16))
            scores = jnp.where(lv == w.bit_length() - 1, p, scores)
            w *= 2
        same = jnp.sum(q * (kfv + kbv), axis=-1, keepdims=True) * v
        return same + _dot(scores.astype(BF16), v.astype(BF16))

    def finish(rows, o):
        o = o * lax.rsqrt(jnp.mean(o * o, -1, keepdims=True) + RMS_EPS)
        o_ref[rows, :] = (o * gs[rows, :]).astype(o_ref.dtype)

    def step(n, carry, *, other_half_done):
        rows = pl.ds(pl.multiple_of(n * C, C), C)
        q, kfv, kbv, bfv, cbv, v = qs[rows, :], kf[rows, :], kb[rows, :], bf[rows, :], cb[rows, :], v_ref[rows, :]
        o = intra(q, kfv, kbv, bfv, cbv, v)
        st = sf[...]
        o = o + _dot_nt((q * jnp.exp2(bfv)).astype(BF16), st.astype(BF16))
        if other_half_done:
            finish(rows, o + ob[rows, :])
        else:
            of[rows, :] = o
        b_last = bfv[C - 1:C, :]
        kt = (kfv * jnp.exp2(b_last - bfv)).astype(BF16)
        sf[...] = jnp.exp2(b_last) * st + _dot_tn(v.astype(BF16), kt)

        rows = pl.ds(pl.multiple_of((n_chunks - 1 - n) * C, C), C)
        q, kbv, cbv, v = qs[rows, :], kb[rows, :], cb[rows, :], v_ref[rows, :]
        st = sb[...]
        o = _dot_nt((q * jnp.exp2(cbv)).astype(BF16), st.astype(BF16))
        if other_half_done:
            finish(rows, o + of[rows, :])
        else:
            ob[rows, :] = o
        c_first = cbv[0:1, :]
        kt = (kbv * jnp.exp2(c_first - cbv)).astype(BF16)
        sb[...] = jnp.exp2(c_first) * st + _dot_tn(v.astype(BF16), kt)
        return carry

    if n_chunks % 2 == 0:
        half = n_chunks // 2
        unroll = HGRN_UNROLL if half % HGRN_UNROLL == 0 else 1
        lax.fori_loop(0, half, functools.partial(step, other_half_done=False), 0, unroll=unroll)
        lax.fori_loop(half, n_chunks, functools.partial(step, other_half_done=True), 0, unroll=unroll)
    else:
        lax.fori_loop(0, n_chunks, functools.partial(step, other_half_done=False), 0)
        for n in range(n_chunks):
            rows = slice(n * C, (n + 1) * C)
            finish(rows, of[rows, :] + ob[rows, :])


def _hgrn_mixer(hg, chunk):
    bsz, seq, _ = hg.shape
    dk = A_HEAD_DIM

    def col(part):
        return pl.BlockSpec((None, seq, dk), lambda b, h, part=part: (b, 0, part * A_HEADS + h))

    body = functools.partial(_hgrn_body, seq=seq, chunk=chunk)
    return pl.pallas_call(
        body,
        grid=(bsz, A_HEADS),
        in_specs=[col(p) for p in range(HG_PLANES)],
        out_specs=pl.BlockSpec((None, seq, dk), lambda b, h: (b, 0, h)),
        out_shape=jax.ShapeDtypeStruct((bsz, seq, A_WIDTH), BF16),
        scratch_shapes=[pltpu.VMEM((seq, dk), F32)] * 2 + [pltpu.VMEM((dk, dk), F32)] * 2
        + [pltpu.VMEM((chunk, chunk), jnp.int32)],
        compiler_params=_cparams("parallel", "arbitrary"),
        name="hgrn2",
    )(*([hg] * HG_PLANES))


def _t5_buckets(rel):
    half = REL_BUCKETS // 2
    max_exact = half // 2
    n = np.abs(rel)
    large = max_exact + (np.log(np.maximum(n, 1) / max_exact) / np.log(REL_MAX_DIST / max_exact)
                         * (half - max_exact)).astype(np.int32)
    large = np.minimum(large, half - 1)
    return (np.where(rel > 0, half, 0) + np.where(n < max_exact, n, large)).astype(np.int32)


def _att_geometry(sub_len, half):
    lq = min(ATT_QBLOCK, sub_len)
    win = min(lq + 2 * half, sub_len)
    return lq, win, sub_len // lq


def _att_bias(rel_bias_g, sub_len, dilation, half):
    lq, win, _ = _att_geometry(sub_len, half)
    p = win + lq - 1
    tables = []
    for delta in (0, -half, lq - win):
        rel = np.arange(p + 1) - (lq - 1) + delta
        valid = (np.abs(rel) <= half) & (np.arange(p + 1) < p)
        u = jnp.where(valid[:, None], rel_bias_g[_t5_buckets(rel * dilation)], NEG_INF).T
        flat = jnp.tile(u, (1, lq))[:, :lq * p]
        tables.append(flat.reshape(-1, lq, p)[:, :, lq - 1:lq - 1 + win])
    return jnp.stack(tables).astype(F32)


def _att_body(q_ref, k_ref, v_ref, bias_ref, o_ref, lse_ref, qd, kd, vd, oacc, *stage, seq, r, lq, win, half):
    sub_len = seq // r
    nq = sub_len // lq
    n_tasks = r * nq
    unroll = min(ATT_UNROLL, n_tasks)
    lanes = q_ref.shape[-1]
    heads = lanes // B_HEAD_DIM
    lane = lax.broadcasted_iota(jnp.int32, (lq, lanes), 1)
    head_lanes = [(lane >= h * B_HEAD_DIM) & (lane < (h + 1) * B_HEAD_DIM) for h in range(heads)]

    if stage:
        r1 = ATT_SPLIT_STRIDE
        len1 = seq // r1
        for src, dst in zip((q_ref, k_ref, v_ref), stage):
            for c1 in range(r1):
                dst[c1 * len1:(c1 + 1) * len1, :] = src[pl.ds(c1, len1, stride=r1), :]
        sources, r2 = stage, r // r1
    else:
        sources, r1, len1, r2 = (q_ref, k_ref, v_ref), 1, seq, r

    def split(c, carry):
        members = pl.ds((c % r1) * len1 + c // r1, sub_len, stride=r2)
        dst = pl.ds(pl.multiple_of(c * sub_len, sub_len), sub_len)
        qd[dst, :] = (sources[0][members, :] * (B_HEAD_DIM ** -0.5)).astype(BF16)
        kd[dst, :] = sources[1][members, :].astype(BF16)
        vd[dst, :] = sources[2][members, :].astype(BF16)
        return carry

    lax.fori_loop(0, r, split, 0)

    def task(tk):
        c = tk // nq
        iq = tk % nq
        q0 = iq * lq
        ws = jnp.clip(q0 - half, 0, sub_len - win)
        case = jnp.where(iq == 0, 0, jnp.where(iq == nq - 1, 2, 1))
        base = c * sub_len
        q = qd[pl.ds(pl.multiple_of(base + q0, lq), lq), :]
        kwin = kd[pl.ds(pl.multiple_of(base + ws, half), win), :]
        vwin = vd[pl.ds(pl.multiple_of(base + ws, half), win), :]
        qm = jnp.concatenate([jnp.where(hl, q, jnp.zeros_like(q)) for hl in head_lanes], axis=0)
        bias = jnp.concatenate([bias_ref[case, h] for h in range(heads)], axis=0)
        s = _dot_nt(qm, kwin) + bias
        m = jnp.max(s, -1, keepdims=True)
        p = jnp.exp(s - m)
        den = jnp.sum(p, -1, keepdims=True)
        o2 = _dot(p.astype(BF16), vwin) / den
        lse2 = m + jnp.log(den)
        out = o2[0:lq]
        lse = jnp.broadcast_to(lse2[0:lq], (lq, lanes))
        for h in range(1, heads):
            out = jnp.where(head_lanes[h], o2[h * lq:(h + 1) * lq], out)
            lse = jnp.where(head_lanes[h], lse2[h * lq:(h + 1) * lq], lse)
        dst = pl.ds(c + q0 * r, lq, stride=r)
        oacc[dst, :] = out
        lse_ref[dst, :] = lse

    def group(i, carry):
        for u in range(unroll):
            task(i * unroll + u)
        return carry

    lax.fori_loop(0, n_tasks // unroll, group, 0)
    o_ref[...] = oacc[...].astype(o_ref.dtype)


def _dilated_group(hb, rel_bias_g, gi, window, dilation):
    bsz, seq, _ = hb.shape
    r = dilation
    sub_len = seq // r
    half = window // (2 * r)
    lq, win, nq = _att_geometry(sub_len, half)
    lanes = ATT_LANES
    pairs = B_GROUP_WIDTH // lanes
    heads = lanes // B_HEAD_DIM
    bias = _att_bias(rel_bias_g, sub_len, r, half)

    def part(p):
        return pl.BlockSpec((None, seq, lanes), lambda b, hp, p=p: (b, 0, (p * len(B_GROUPS) + gi) * pairs + hp))

    def out_spec():
        return pl.BlockSpec((None, seq, lanes), lambda b, hp: (b, 0, hp))

    body = functools.partial(_att_body, seq=seq, r=r, lq=lq, win=win, half=half)
    return pl.pallas_call(
        body,
        grid=(bsz, pairs),
        in_specs=[part(0), part(1), part(2),
                  pl.BlockSpec((3, heads, lq, win), lambda b, hp: (0, hp, 0, 0))],
        out_specs=[out_spec(), out_spec()],
        out_shape=[jax.ShapeDtypeStruct((bsz, seq, B_GROUP_WIDTH), BF16),
                   jax.ShapeDtypeStruct((bsz, seq, B_GROUP_WIDTH), F32)],
        scratch_shapes=[pltpu.VMEM((seq, lanes), BF16)] * 3 + [pltpu.VMEM((seq, lanes), F32)]
        + ([pltpu.VMEM((seq, lanes), F32)] * 3 if r > ATT_SPLIT_STRIDE else []),
        compiler_params=_cparams("parallel", "arbitrary"),
        name=f"dilated_attn_r{r}",
    )(hb, hb, hb, bias)


def _ab_mix_rows(rows, oa_ref, o_refs, lse_refs, x_ref, w_ref, g_ref, b_ref):
    lses = [r[rows, :] for r in lse_refs]
    m = functools.reduce(jnp.maximum, lses)
    es = [jnp.exp(l - m) for l in lses]
    tot = functools.reduce(jnp.add, es)
    mix = _dot(oa_ref[rows, :], w_ref[0:A_WIDTH, :])
    for gi, (o_ref, e) in enumerate(zip(o_refs, es)):
        og = (o_ref[rows, :].astype(F32) * (e / tot)).astype(BF16)
        lo = A_WIDTH + gi * B_GROUP_WIDTH
        mix = mix + _dot(og, w_ref[lo:lo + B_GROUP_WIDTH, :])
    return _layer_norm(ALPHA * x_ref[rows, :] + mix, g_ref[...], b_ref[...])


def _mlp_rows(x, w1_ref, w2_ref, g_ref, b_ref):
    h = jnp.square(jnp.maximum(_dot(x.astype(BF16), w1_ref[...]), 0.0)).astype(BF16)
    return _layer_norm(ALPHA * x + _dot(h, w2_ref[...]), g_ref[...], b_ref[...])


def _ab_out_mlp_body(oa_ref, o1_ref, o2_ref, o3_ref, l1_ref, l2_ref, l3_ref, x_ref, wo_ref, g1_ref, b1_ref,
                     w1_ref, w2_ref, g2_ref, b2_ref, y_ref, *, sub):
    for r0 in range(0, x_ref.shape[0], sub):
        rows = slice(r0, r0 + sub)
        x1 = _ab_mix_rows(rows, oa_ref, (o1_ref, o2_ref, o3_ref), (l1_ref, l2_ref, l3_ref), x_ref, wo_ref,
                          g1_ref, b1_ref)
        y_ref[rows, :] = _mlp_rows(x1, w1_ref, w2_ref, g2_ref, b2_ref)


def _ab_out_mlp(oa, outs, lses, x, w_out, ln1_g, ln1_b, w1, w2, ln2_g, ln2_b, tm, sub):
    t = x.shape[0]

    def rowblock(width):
        return pl.BlockSpec((tm, width), lambda i: (i, 0))

    def vec():
        return _resident((1, D_MODEL))

    body = functools.partial(_ab_out_mlp_body, sub=sub)
    return pl.pallas_call(
        body,
        grid=(t // tm,),
        in_specs=[rowblock(A_WIDTH)] + [rowblock(B_GROUP_WIDTH)] * 6
        + [rowblock(D_MODEL), _resident(w_out.shape), vec(), vec(), _resident(w1.shape), _resident(w2.shape),
           vec(), vec()],
        out_specs=rowblock(D_MODEL),
        out_shape=jax.ShapeDtypeStruct((t, D_MODEL), F32),
        compiler_params=_cparams("parallel"),
        name="ab_out_mlp_ln",
    )(oa, *outs, *lses, x, w_out, ln1_g.reshape(1, D_MODEL), ln1_b.reshape(1, D_MODEL), w1, w2,
      ln2_g.reshape(1, D_MODEL), ln2_b.reshape(1, D_MODEL))


def _mlp_body(x_ref, w1_ref, w2_ref, g_ref, b_ref, y_ref, *, sub):
    for r0 in range(0, x_ref.shape[0], sub):
        rows = slice(r0, r0 + sub)
        y_ref[rows, :] = _mlp_rows(x_ref[rows, :], w1_ref, w2_ref, g_ref, b_ref)


def _mlp(x, w1, w2, ln_g, ln_b, tm, sub):
    t = x.shape[0]
    body = functools.partial(_mlp_body, sub=sub)
    return pl.pallas_call(
        body,
        grid=(t // tm,),
        in_specs=[pl.BlockSpec((tm, D_MODEL), lambda i: (i, 0)),
                  _resident(w1.shape), _resident(w2.shape),
                  _resident((1, D_MODEL)), _resident((1, D_MODEL))],
        out_specs=pl.BlockSpec((tm, D_MODEL), lambda i: (i, 0)),
        out_shape=jax.ShapeDtypeStruct((t, D_MODEL), F32),
        compiler_params=_cparams("parallel"),
        name="mlp_ln",
    )(x, w1, w2, ln_g.reshape(1, D_MODEL), ln_b.reshape(1, D_MODEL))


def _glu_body(x_ref, w_ref, b_ref, u_ref, *, tn):
    xb = x_ref[...].astype(BF16)
    for c0 in range(0, D_MODEL, tn):
        a = _dot(xb, w_ref[:, c0:c0 + tn]) + b_ref[:, c0:c0 + tn]
        gate = _dot(xb, w_ref[:, D_MODEL + c0:D_MODEL + c0 + tn]) + b_ref[:, D_MODEL + c0:D_MODEL + c0 + tn]
        u_ref[:, c0:c0 + tn] = a * jax.nn.sigmoid(gate)


def _glu_proj(x, w_in, b_in, tm, tn):
    t = x.shape[0]
    body = functools.partial(_glu_body, tn=tn)
    return pl.pallas_call(
        body,
        grid=(t // tm,),
        in_specs=[pl.BlockSpec((tm, D_MODEL), lambda i: (i, 0)),
                  _resident((D_MODEL, 2 * D_MODEL)), _resident((1, 2 * D_MODEL))],
        out_specs=pl.BlockSpec((tm, D_MODEL), lambda i: (i, 0)),
        out_shape=jax.ShapeDtypeStruct((t, D_MODEL), F32),
        compiler_params=_cparams("parallel"),
        name="conv_glu",
    )(x, w_in, b_in.reshape(1, 2 * D_MODEL))


def _conv_out_body(u_ref, up_ref, un_ref, x_ref, dw_ref, dwb_ref, ng_ref, nb_ref, w_ref, bo_ref, g_ref, b_ref,
                   y_ref, buf, act, *, ts):
    i = pl.program_id(1)
    halo = CONV_HALO
    pad = CONV_WIDTH // 2
    for s in range(D_MODEL // CONV_LANES):
        cs = slice(s * CONV_LANES, (s + 1) * CONV_LANES)
        buf[s, 0:halo, :] = jnp.where(i > 0, up_ref[0, :, cs], 0.0)
        buf[s, halo:halo + ts, :] = u_ref[0, :, cs]
        buf[s, halo + ts:halo + ts + halo, :] = jnp.where(i < pl.num_programs(1) - 1, un_ref[0, :, cs], 0.0)
    dwb, ng, nb = dwb_ref[...], ng_ref[...], nb_ref[...]
    n_tiles = ts // CONV_ROWS

    def conv_tile(t):
        base = pl.multiple_of(t * CONV_ROWS, CONV_ROWS)
        n_groups = CONV_ROWS // SUBLANES
        cols = []
        for s in range(D_MODEL // CONV_LANES):
            lanes = slice(s * CONV_LANES, (s + 1) * CONV_LANES)
            accs = [jnp.zeros((SUBLANES, CONV_LANES), F32) for _ in range(n_groups)]
            for o in range(halo - pad, CONV_ROWS + halo + pad - SUBLANES + 1):
                data = buf[s, pl.ds(base + o, SUBLANES, stride=1), :]
                for g in range(n_groups):
                    j = o - (halo - pad) - SUBLANES * g
                    if 0 <= j < CONV_WIDTH:
                        accs[g] = accs[g] + dw_ref[j:j + 1, lanes] * data
            cols.append(jnp.concatenate(accs, axis=0))
        c = _layer_norm(jnp.concatenate(cols, axis=1) + dwb, ng, nb)
        act[t % 2] = (c * jax.nn.sigmoid(c)).astype(BF16)

    def out_tile(t):
        rows = pl.ds(pl.multiple_of(t * CONV_ROWS, CONV_ROWS), CONV_ROWS)
        mix = _dot(act[t % 2], w_ref[...]) + bo_ref[...]
        y_ref[0, rows, :] = _layer_norm(ALPHA * x_ref[0, rows, :] + mix, g_ref[...], b_ref[...])

    conv_tile(0)

    def pipelined(t, carry):
        conv_tile(t)
        out_tile(t - 1)
        return carry

    lax.fori_loop(1, n_tiles, pipelined, 0)
    out_tile(n_tiles - 1)


def _conv_out(u, x, dw, dw_b, n_g, n_b, w_out, b_out, ln_g, ln_b, ts):
    bsz, seq, d = x.shape
    hb = ts // CONV_HALO
    last_halo = seq // CONV_HALO - 1

    def vec():
        return _resident((1, d))

    body = functools.partial(_conv_out_body, ts=ts)
    return pl.pallas_call(
        body,
        grid=(bsz, seq // ts),
        in_specs=[pl.BlockSpec((1, ts, d), lambda b, i: (b, i, 0)),
                  pl.BlockSpec((1, CONV_HALO, d), lambda b, i: (b, jnp.maximum(i * hb - 1, 0), 0)),
                  pl.BlockSpec((1, CONV_HALO, d), lambda b, i: (b, jnp.minimum((i + 1) * hb, last_halo), 0)),
                  pl.BlockSpec((1, ts, d), lambda b, i: (b, i, 0)),
                  _resident((CONV_WIDTH, d)),
                  vec(), vec(), vec(),
                  _resident((d, d)),
                  vec(), vec(), vec()],
        out_specs=pl.BlockSpec((1, ts, d), lambda b, i: (b, i, 0)),
        out_shape=jax.ShapeDtypeStruct((bsz, seq, d), F32),
        scratch_shapes=[pltpu.VMEM((d // CONV_LANES, ts + 2 * CONV_HALO, CONV_LANES), F32),
                        pltpu.VMEM((2, CONV_ROWS, d), BF16)],
        compiler_params=_cparams("parallel", "arbitrary"),
        name="conv_out_ln",
    )(u, u, u, x, dw, dw_b.reshape(1, d), n_g.reshape(1, d), n_b.reshape(1, d), w_out,
      b_out.reshape(1, d), ln_g.reshape(1, d), ln_b.reshape(1, d))


def _trunk(x, rel_bias, hgrn_lb, w_in_ab, hgrn_norm, w_out_ab, w_in_c, b_in_c, dw_c, dw_b_c, cnorm_g, cnorm_b,
           w_out_c, b_out_c, ln_mix_g, ln_mix_b, mlp_w1, mlp_w2, ln_ffn_g, ln_ffn_b):
    bsz, seq, d = x.shape
    t = bsz * seq
    x2 = x.reshape(t, d)
    for l in range(DEPTH):
        i = l // 2
        w1, w2 = mlp_w1[l].astype(BF16), mlp_w2[l].astype(BF16)
        if l % 2 == 0:
            wa = w_in_ab[i][:, :5 * A_WIDTH].reshape(d, 5, A_HEADS, A_HEAD_DIM)
            wa = wa.transpose(0, 2, 1, 3).reshape(d, 5 * A_WIDTH).astype(BF16)
            wb = w_in_ab[i][:, 5 * A_WIDTH:].astype(BF16)
            chunk = min(HGRN_CHUNK, seq)
            hg, hb = _in_proj(x2, wa, wb, hgrn_lb, hgrn_norm[i], min(IN_PROJ_ROWS, seq), chunk, l)
            oa = _hgrn_mixer(hg.reshape(bsz, seq, HG_PLANES * A_WIDTH), chunk)
            hb3 = hb.reshape(bsz, seq, 3 * B_WIDTH)
            outs, lses = [], []
            for gi, (window, r) in enumerate(B_GROUPS):
                hs = slice(gi * B_HEADS_PER_GROUP, (gi + 1) * B_HEADS_PER_GROUP)
                o, lse = _dilated_group(hb3, rel_bias[:, hs], gi, window, r)
                outs.append(o.reshape(t, B_GROUP_WIDTH))
                lses.append(lse.reshape(t, B_GROUP_WIDTH))
            x2 = _ab_out_mlp(oa.reshape(t, A_WIDTH), outs, lses, x2, w_out_ab[i].astype(BF16),
                             ln_mix_g[l], ln_mix_b[l], w1, w2, ln_ffn_g[l], ln_ffn_b[l],
                             min(AB_MLP_ROWS, t), min(MLP_SUB_ROWS, t))
        else:
            u = _glu_proj(x2, w_in_c[i].astype(BF16), b_in_c[i], min(GLU_ROWS, t), GLU_COLS)
            x2 = _conv_out(u.reshape(bsz, seq, d), x2.reshape(bsz, seq, d), dw_c[i], dw_b_c[i], cnorm_g[i],
                           cnorm_b[i], w_out_c[i].astype(BF16), b_out_c[i], ln_mix_g[l], ln_mix_b[l],
                           min(CONV_BLOCK, seq)).reshape(t, d)
            x2 = _mlp(x2, w1, w2, ln_ffn_g[l], ln_ffn_b[l], min(MLP_ROWS, t), min(MLP_SUB_ROWS, t))
    return x2.reshape(bsz, seq, d)


def kernel(x_prompt, x_sample, rel_bias, hgrn_lb, w_in_ab, hgrn_norm, w_out_ab, w_in_c, b_in_c, dw_c, dw_b_c,
           cnorm_g, cnorm_b, w_out_c, b_out_c, ln_mix_g, ln_mix_b, mlp_w1, mlp_w2, ln_ffn_g, ln_ffn_b):
    params = (rel_bias, hgrn_lb, w_in_ab, hgrn_norm, w_out_ab, w_in_c, b_in_c, dw_c, dw_b_c, cnorm_g, cnorm_b,
              w_out_c, b_out_c, ln_mix_g, ln_mix_b, mlp_w1, mlp_w2, ln_ffn_g, ln_ffn_b)
    return (_trunk(x_prompt, *params), _trunk(x_sample, *params))
```

```python
import functools

import numpy as np
import jax
import jax.numpy as jnp
from jax import lax
from jax.experimental import pallas as pl
from jax.experimental.pallas import tpu as pltpu

F32 = jnp.float32
BF16 = jnp.bfloat16

D_MODEL = 1024
DEPTH = 2
A_HEADS = 4
A_HEAD_DIM = 128
A_WIDTH = A_HEADS * A_HEAD_DIM
B_GROUPS = ((128, 1), (512, 4), (2048, 16))
B_HEADS_PER_GROUP = 4
B_HEAD_DIM = 64
B_GROUP_WIDTH = B_HEADS_PER_GROUP * B_HEAD_DIM
B_WIDTH = B_GROUP_WIDTH * len(B_GROUPS)
CONV_WIDTH = 31
D_FF = 4 * D_MODEL
REL_BUCKETS = 32
REL_MAX_DIST = 1024
ALPHA = (2 * DEPTH) ** 0.25
LN_EPS = 1e-5
RMS_EPS = 1e-6
NEG_INF = -1e30

SUBLANES = 8
IN_PROJ_ROWS = 512
HGRN_CHUNK = 128
HGRN_UNROLL = 8
ATT_QBLOCK = 128
ATT_LANES = 128
ATT_UNROLL = 16
ATT_SPLIT_STRIDE = 4
MLP_ROWS = 1024
AB_OUT_ROWS = 1024
MLP_SUB_ROWS = 256
GLU_ROWS = 1024
GLU_COLS = 512
CONV_BLOCK = 1024
CONV_HALO = 16
CONV_ROWS = 128
CONV_LANES = 128
VMEM_LIMIT = 52 * 1024 * 1024


def _cparams(*sem):
    return pltpu.CompilerParams(dimension_semantics=sem, vmem_limit_bytes=VMEM_LIMIT)


def _resident(shape):
    return pl.BlockSpec(shape, lambda *_: (0,) * len(shape), pipeline_mode=pl.Buffered(1))


def _dot(a, b):
    return jnp.dot(a, b, preferred_element_type=F32)


def _dot_nt(a, b):
    return lax.dot_general(a, b, (((1,), (1,)), ((), ())), preferred_element_type=F32)


def _dot_tn(a, b):
    return lax.dot_general(a, b, (((0,), (0,)), ((), ())), preferred_element_type=F32)


def _layer_norm(y, g, b):
    mu = jnp.mean(y, -1, keepdims=True)
    yc = y - mu
    var = jnp.mean(jnp.square(yc), -1, keepdims=True)
    return yc * lax.rsqrt(var + LN_EPS) * g + b


HG_PLANES = 7


def _in_proj_body(x_ref, wa_ref, wb_ref, lbp_ref, gn_ref, hg_ref, hb_ref, ha, *, chunk, layer):
    tm = x_ref.shape[0]
    dk = A_HEAD_DIM
    xb = x_ref[...].astype(BF16)
    n_groups = chunk // SUBLANES
    grp = (n_groups, SUBLANES, dk)
    sub = lax.broadcasted_iota(jnp.int32, grp, 1)

    def lower_bound(direction, lanes):
        p = [lbp_ref[direction, l:l + 1, lanes] for l in range(DEPTH + 1)]
        m = functools.reduce(jnp.maximum, p)
        e = [jnp.exp(v - m) for v in p]
        return sum(e[:layer + 1]) / sum(e)

    def gates(z, lb_d):
        sig = jax.nn.sigmoid(z)
        log2f = jnp.log2(lb_d + (1.0 - lb_d) * sig)
        k = (1.0 - lb_d) * (1.0 - sig)
        return k, log2f

    def chunk_scan(x, reverse):
        x3 = x.reshape(grp)
        s = 1
        while s < SUBLANES:
            if reverse:
                x3 = x3 + jnp.where(sub < SUBLANES - s, pltpu.roll(x3, SUBLANES - s, 1), 0.0)
            else:
                x3 = x3 + jnp.where(sub >= s, pltpu.roll(x3, s, 1), 0.0)
            s *= 2
        edge = 0 if reverse else SUBLANES - 1
        order = range(n_groups - 1, -1, -1) if reverse else range(n_groups)
        run = jnp.zeros((1, 1, dk), F32)
        offsets = [None] * n_groups
        for g in order:
            offsets[g] = run
            run = run + x3[g:g + 1, edge:edge + 1, :]
        return (x3 + jnp.concatenate(offsets, axis=0)).reshape(chunk, dk)

    def plane(p, h):
        return slice(p * A_WIDTH + h * dk, p * A_WIDTH + (h + 1) * dk)

    head_cols = wa_ref.shape[1] // A_HEADS
    for hp in range(A_HEADS // 2):
        ha[hp] = _dot(xb, wa_ref[:, hp * 2 * head_cols:(hp + 1) * 2 * head_cols])
    hb_ref[...] = _dot(xb, wb_ref[...])
    for h in range(A_HEADS):
        lanes = slice(h * dk, (h + 1) * dk)
        lb_f, lb_b = lower_bound(0, lanes), lower_bound(1, lanes)
        gn = gn_ref[:, lanes]

        def part(p, rows):
            c0 = (h % 2) * head_cols + p * dk
            return ha[h // 2, rows, c0:c0 + dk]

        for r0 in range(0, tm, chunk):
            rows = slice(r0, r0 + chunk)
            q = part(0, rows)
            hg_ref[rows, plane(0, h)] = q * jax.nn.sigmoid(q)
            hg_ref[rows, plane(1, h)] = part(1, rows)
            k, lf = gates(part(2, rows), lb_f)
            hg_ref[rows, plane(2, h)] = k
            hg_ref[rows, plane(4, h)] = chunk_scan(lf, reverse=False)
            k, lg = gates(part(3, rows), lb_b)
            hg_ref[rows, plane(3, h)] = k
            hg_ref[rows, plane(5, h)] = chunk_scan(lg, reverse=True)
            g = part(4, rows)
            hg_ref[rows, plane(6, h)] = gn * (g * jax.nn.sigmoid(g))


def _in_proj(x, wa, wb, hgrn_lb, g_norm, tm, chunk, layer):
    m, k = x.shape
    na, nb = wa.shape[1], wb.shape[1]
    body = functools.partial(_in_proj_body, chunk=chunk, layer=layer)
    return pl.pallas_call(
        body,
        grid=(m // tm,),
        in_specs=[pl.BlockSpec((tm, k), lambda i: (i, 0)),
                  _resident((k, na)), _resident((k, nb)), _resident(hgrn_lb.shape), _resident((1, A_WIDTH))],
        out_specs=[pl.BlockSpec((tm, HG_PLANES * A_WIDTH), lambda i: (i, 0)),
                   pl.BlockSpec((tm, nb), lambda i: (i, 0))],
        out_shape=[jax.ShapeDtypeStruct((m, HG_PLANES * A_WIDTH), F32), jax.ShapeDtypeStruct((m, nb), F32)],
        scratch_shapes=[pltpu.VMEM((A_HEADS // 2, tm, 2 * na // A_HEADS), F32)],
        compiler_params=_cparams("parallel"),
        name="in_proj",
    )(x, wa, wb, hgrn_lb, g_norm.reshape(1, A_WIDTH))


def _hgrn_body(qs, v_ref, kf, kb, bf, cb, gs, o_ref, of, ob, sf, sb, level, *, seq, chunk):
    C = chunk
    n_chunks = seq // C
    dk = A_HEAD_DIM
    grp = (C // SUBLANES, SUBLANES, dk)

    sf[...] = jnp.zeros_like(sf)
    sb[...] = jnp.zeros_like(sb)
    sub = lax.broadcasted_iota(jnp.int32, grp, 1)
    level[...] = 31 - lax.clz(lax.broadcasted_iota(jnp.int32, (C, C), 0) ^ lax.broadcasted_iota(jnp.int32, (C, C), 1))

    def mid_row(x3, w):
        if w == 1:
            return jnp.where((sub & 1) == 1, x3, pltpu.roll(x3, SUBLANES - 1, 1))
        if w == 2:
            return jnp.where(sub < 4, jnp.broadcast_to(x3[:, 2:3, :], grp), jnp.broadcast_to(x3[:, 6:7, :], grp))
        return jnp.broadcast_to(x3[:, 4:5, :], grp)

    def intra(q, kfv, kbv, bfv, cbv, v):
        lv = level[...]
        scores = jnp.zeros((C, C), F32)
        q3 = q.reshape(grp)
        kf3, kb3, bf3, cb3 = kfv.reshape(grp), kbv.reshape(grp), bfv.reshape(grp), cbv.reshape(grp)
        w = 1
        while w < C:
            if w < SUBLANES:
                second = (sub & w) != 0
                d_f = bf3 - mid_row(bf3, w)
                d_b = cb3 - mid_row(cb3, w)
                qt = (q3 * jnp.exp2(jnp.where(second, d_f, d_b))).reshape(C, dk)
                kt = (jnp.where(second, kb3, kf3) * jnp.exp2(-jnp.where(second, d_b, d_f))).reshape(C, dk)
            else:
                qparts, kparts = [], []
                for r0 in range(0, C, 2 * w):
                    first, second = slice(r0, r0 + w), slice(r0 + w, r0 + 2 * w)
                    bf_r, cb_r = bfv[r0 + w:r0 + w + 1, :], cbv[r0 + w:r0 + w + 1, :]
                    qparts += [q[first] * jnp.exp2(cbv[first] - cb_r), q[second] * jnp.exp2(bfv[second] - bf_r)]
                    kparts += [kfv[first] * jnp.exp2(bf_r - bfv[first]), kbv[second] * jnp.exp2(cb_r - cbv[second])]
                qt, kt = jnp.concatenate(qparts, 0), jnp.concatenate(kparts, 0)
            p = _dot_nt(qt.astype(BF16), kt.astype(BF16))
            scores = jnp.where(lv == w.bit_length() - 1, p, scores)
            w *= 2
        same = jnp.sum(q * (kfv + kbv), axis=-1, keepdims=True) * v
        return same + _dot(scores.astype(BF16), v.astype(BF16))

    def finish(rows, o):
        o = o * lax.rsqrt(jnp.mean(o * o, -1, keepdims=True) + RMS_EPS)
        o_ref[rows, :] = (o * gs[rows, :]).astype(o_ref.dtype)

    def step(n, carry, *, other_half_done):
        rows = pl.ds(pl.multiple_of(n * C, C), C)
        q, kfv, kbv, bfv, cbv, v = qs[rows, :], kf[rows, :], kb[rows, :], bf[rows, :], cb[rows, :], v_ref[rows, :]
        o = intra(q, kfv, kbv, bfv, cbv, v)
        st = sf[...]
        o = o + _dot_nt((q * jnp.exp2(bfv)).astype(BF16), st.astype(BF16))
        if other_half_done:
            finish(rows, o + ob[rows, :])
        else:
            of[rows, :] = o
        b_last = bfv[C - 1:C, :]
        kt = (kfv * jnp.exp2(b_last - bfv)).astype(BF16)
        sf[...] = jnp.exp2(b_last) * st + _dot_tn(v.astype(BF16), kt)

        rows = pl.ds(pl.multiple_of((n_chunks - 1 - n) * C, C), C)
        q, kbv, cbv, v = qs[rows, :], kb[rows, :], cb[rows, :], v_ref[rows, :]
        st = sb[...]
        o = _dot_nt((q * jnp.exp2(cbv)).astype(BF16), st.astype(BF16))
        if other_half_done:
            finish(rows, o + of[rows, :])
        else:
            ob[rows, :] = o
        c_first = cbv[0:1, :]
        kt = (kbv * jnp.exp2(c_first - cbv)).astype(BF16)
        sb[...] = jnp.exp2(c_first) * st + _dot_tn(v.astype(BF16), kt)
        return carry

    if n_chunks % 2 == 0:
        half = n_chunks // 2
        unroll = HGRN_UNROLL if half % HGRN_UNROLL == 0 else 1
        lax.fori_loop(0, half, functools.partial(step, other_half_done=False), 0, unroll=unroll)
        lax.fori_loop(half, n_chunks, functools.partial(step, other_half_done=True), 0, unroll=unroll)
    else:
        lax.fori_loop(0, n_chunks, functools.partial(step, other_half_done=False), 0)
        for n in range(n_chunks):
            rows = slice(n * C, (n + 1) * C)
            finish(rows, of[rows, :] + ob[rows, :])


def _hgrn_mixer(hg, chunk):
    bsz, seq, _ = hg.shape
    dk = A_HEAD_DIM

    def col(part):
        return pl.BlockSpec((None, seq, dk), lambda b, h, part=part: (b, 0, part * A_HEADS + h))

    body = functools.partial(_hgrn_body, seq=seq, chunk=chunk)
    return pl.pallas_call(
        body,
        grid=(bsz, A_HEADS),
        in_specs=[col(p) for p in range(HG_PLANES)],
        out_specs=pl.BlockSpec((None, seq, dk), lambda b, h: (b, 0, h)),
        out_shape=jax.ShapeDtypeStruct((bsz, seq, A_WIDTH), BF16),
        scratch_shapes=[pltpu.VMEM((seq, dk), F32)] * 2 + [pltpu.VMEM((dk, dk), F32)] * 2
        + [pltpu.VMEM((chunk, chunk), jnp.int32)],
        compiler_params=_cparams("parallel", "arbitrary"),
        name="hgrn2",
    )(*([hg] * HG_PLANES))


def _t5_buckets(rel):
    half = REL_BUCKETS // 2
    max_exact = half // 2
    n = np.abs(rel)
    large = max_exact + (np.log(np.maximum(n, 1) / max_exact) / np.log(REL_MAX_DIST / max_exact)
                         * (half - max_exact)).astype(np.int32)
    large = np.minimum(large, half - 1)
    return (np.where(rel > 0, half, 0) + np.where(n < max_exact, n, large)).astype(np.int32)


def _att_geometry(sub_len, half):
    lq = min(ATT_QBLOCK, sub_len)
    win = min(lq + 2 * half, sub_len)
    return lq, win, sub_len // lq


def _att_bias(rel_bias_g, sub_len, dilation, half):
    lq, win, _ = _att_geometry(sub_len, half)
    p = win + lq - 1
    tables = []
    for delta in (0, -half, lq - win):
        rel = np.arange(p + 1) - (lq - 1) + delta
        valid = (np.abs(rel) <= half) & (np.arange(p + 1) < p)
        u = jnp.where(valid[:, None], rel_bias_g[_t5_buckets(rel * dilation)], NEG_INF).T
        flat = jnp.tile(u, (1, lq))[:, :lq * p]
        tables.append(flat.reshape(-1, lq, p)[:, :, lq - 1:lq - 1 + win])
    return jnp.stack(tables).astype(F32)


def _att_body(q_ref, k_ref, v_ref, bias_ref, o_ref, lse_ref, qd, kd, vd, oacc, *stage, seq, r, lq, win, half):
    sub_len = seq // r
    nq = sub_len // lq
    n_tasks = r * nq
    unroll = min(ATT_UNROLL, n_tasks)
    lanes = q_ref.shape[-1]
    heads = lanes // B_HEAD_DIM
    lane = lax.broadcasted_iota(jnp.int32, (lq, lanes), 1)
    head_lanes = [(lane >= h * B_HEAD_DIM) & (lane < (h + 1) * B_HEAD_DIM) for h in range(heads)]

    if stage:
        r1 = ATT_SPLIT_STRIDE
        len1 = seq // r1
        for src, dst in zip((q_ref, k_ref, v_ref), stage):
            for c1 in range(r1):
                dst[c1 * len1:(c1 + 1) * len1, :] = src[pl.ds(c1, len1, stride=r1), :]
        sources, r2 = stage, r // r1
    else:
        sources, r1, len1, r2 = (q_ref, k_ref, v_ref), 1, seq, r

    def split(c, carry):
        members = pl.ds((c % r1) * len1 + c // r1, sub_len, stride=r2)
        dst = pl.ds(pl.multiple_of(c * sub_len, sub_len), sub_len)
        qd[dst, :] = (sources[0][members, :] * (B_HEAD_DIM ** -0.5)).astype(BF16)
        kd[dst, :] = sources[1][members, :].astype(BF16)
        vd[dst, :] = sources[2][members, :].astype(BF16)
        return carry

    lax.fori_loop(0, r, split, 0)

    def task(tk):
        c = tk // nq
        iq = tk % nq
        q0 = iq * lq
        ws = jnp.clip(q0 - half, 0, sub_len - win)
        case = jnp.where(iq == 0, 0, jnp.where(iq == nq - 1, 2, 1))
        base = c * sub_len
        q = qd[pl.ds(pl.multiple_of(base + q0, lq), lq), :]
        kwin = kd[pl.ds(pl.multiple_of(base + ws, half), win), :]
        vwin = vd[pl.ds(pl.multiple_of(base + ws, half), win), :]
        qm = jnp.concatenate([jnp.where(hl, q, jnp.zeros_like(q)) for hl in head_lanes], axis=0)
        bias = jnp.concatenate([bias_ref[case, h] for h in range(heads)], axis=0)
        s = _dot_nt(qm, kwin) + bias
        m = jnp.max(s, -1, keepdims=True)
        p = jnp.exp(s - m)
        den = jnp.sum(p, -1, keepdims=True)
        o2 = _dot(p.astype(BF16), vwin) / den
        lse2 = m + jnp.log(den)
        out = o2[0:lq]
        lse = jnp.broadcast_to(lse2[0:lq], (lq, lanes))
        for h in range(1, heads):
            out = jnp.where(head_lanes[h], o2[h * lq:(h + 1) * lq], out)
            lse = jnp.where(head_lanes[h], lse2[h * lq:(h + 1) * lq], lse)
        dst = pl.ds(c + q0 * r, lq, stride=r)
        oacc[dst, :] = out
        lse_ref[dst, :] = lse

    def group(i, carry):
        for u in range(unroll):
            task(i * unroll + u)
        return carry

    lax.fori_loop(0, n_tasks // unroll, group, 0)
    o_ref[...] = oacc[...].astype(o_ref.dtype)


def _dilated_group(hb, rel_bias_g, gi, window, dilation):
    bsz, seq, _ = hb.shape
    r = dilation
    sub_len = seq // r
    half = window // (2 * r)
    lq, win, nq = _att_geometry(sub_len, half)
    lanes = ATT_LANES
    pairs = B_GROUP_WIDTH // lanes
    heads = lanes // B_HEAD_DIM
    bias = _att_bias(rel_bias_g, sub_len, r, half)

    def part(p):
        return pl.BlockSpec((None, seq, lanes), lambda b, hp, p=p: (b, 0, (p * len(B_GROUPS) + gi) * pairs + hp))

    def out_spec():
        return pl.BlockSpec((None, seq, lanes), lambda b, hp: (b, 0, hp))

    body = functools.partial(_att_body, seq=seq, r=r, lq=lq, win=win, half=half)
    return pl.pallas_call(
        body,
        grid=(bsz, pairs),
        in_specs=[part(0), part(1), part(2),
                  pl.BlockSpec((3, heads, lq, win), lambda b, hp: (0, hp, 0, 0))],
        out_specs=[out_spec(), out_spec()],
        out_shape=[jax.ShapeDtypeStruct((bsz, seq, B_GROUP_WIDTH), BF16),
                   jax.ShapeDtypeStruct((bsz, seq, B_GROUP_WIDTH), F32)],
        scratch_shapes=[pltpu.VMEM((seq, lanes), BF16)] * 3 + [pltpu.VMEM((seq, lanes), F32)]
        + ([pltpu.VMEM((seq, lanes), F32)] * 3 if r > ATT_SPLIT_STRIDE else []),
        compiler_params=_cparams("parallel", "arbitrary"),
        name=f"dilated_attn_r{r}",
    )(hb, hb, hb, bias)


def _ab_out_body(oa_ref, o1_ref, o2_ref, o3_ref, l1_ref, l2_ref, l3_ref, x_ref, w_ref, g_ref, b_ref, y_ref):
    lses = [l1_ref[...], l2_ref[...], l3_ref[...]]
    m = functools.reduce(jnp.maximum, lses)
    es = [jnp.exp(l - m) for l in lses]
    tot = functools.reduce(jnp.add, es)
    mix = _dot(oa_ref[...], w_ref[0:A_WIDTH, :])
    for gi, (o_ref, e) in enumerate(zip((o1_ref, o2_ref, o3_ref), es)):
        og = (o_ref[...].astype(F32) * (e / tot)).astype(BF16)
        lo = A_WIDTH + gi * B_GROUP_WIDTH
        mix = mix + _dot(og, w_ref[lo:lo + B_GROUP_WIDTH, :])
    y_ref[...] = _layer_norm(ALPHA * x_ref[...] + mix, g_ref[...], b_ref[...])


def _ab_out(oa, outs, lses, x, w_out, ln_g, ln_b, tm):
    t = x.shape[0]

    def rowblock(width):
        return pl.BlockSpec((tm, width), lambda i: (i, 0))

    return pl.pallas_call(
        _ab_out_body,
        grid=(t // tm,),
        in_specs=[rowblock(A_WIDTH)] + [rowblock(B_GROUP_WIDTH)] * 6
        + [rowblock(D_MODEL), _resident(w_out.shape), _resident((1, D_MODEL)), _resident((1, D_MODEL))],
        out_specs=rowblock(D_MODEL),
        out_shape=jax.ShapeDtypeStruct((t, D_MODEL), F32),
        compiler_params=_cparams("parallel"),
        name="ab_out_ln",
    )(oa, *outs, *lses, x, w_out, ln_g.reshape(1, D_MODEL), ln_b.reshape(1, D_MODEL))


def _mlp_body(x_ref, w1_ref, w2_ref, g_ref, b_ref, y_ref, *, sub):
    for r0 in range(0, x_ref.shape[0], sub):
        rows = slice(r0, r0 + sub)
        x = x_ref[rows, :]
        h = jnp.square(jnp.maximum(_dot(x.astype(BF16), w1_ref[...]), 0.0)).astype(BF16)
        y_ref[rows, :] = _layer_norm(ALPHA * x + _dot(h, w2_ref[...]), g_ref[...], b_ref[...])


def _mlp(x, w1, w2, ln_g, ln_b, tm, sub):
    t = x.shape[0]
    body = functools.partial(_mlp_body, sub=sub)
    return pl.pallas_call(
        body,
        grid=(t // tm,),
        in_specs=[pl.BlockSpec((tm, D_MODEL), lambda i: (i, 0)),
                  _resident(w1.shape), _resident(w2.shape),
                  _resident((1, D_MODEL)), _resident((1, D_MODEL))],
        out_specs=pl.BlockSpec((tm, D_MODEL), lambda i: (i, 0)),
        out_shape=jax.ShapeDtypeStruct((t, D_MODEL), F32),
        compiler_params=_cparams("parallel"),
        name="mlp_ln",
    )(x, w1, w2, ln_g.reshape(1, D_MODEL), ln_b.reshape(1, D_MODEL))


def _glu_body(x_ref, w_ref, b_ref, u_ref, *, tn):
    xb = x_ref[...].astype(BF16)
    for c0 in range(0, D_MODEL, tn):
        a = _dot(xb, w_ref[:, c0:c0 + tn]) + b_ref[:, c0:c0 + tn]
        gate = _dot(xb, w_ref[:, D_MODEL + c0:D_MODEL + c0 + tn]) + b_ref[:, D_MODEL + c0:D_MODEL + c0 + tn]
        u_ref[:, c0:c0 + tn] = a * jax.nn.sigmoid(gate)


def _glu_proj(x, w_in, b_in, tm, tn):
    t = x.shape[0]
    body = functools.partial(_glu_body, tn=tn)
    return pl.pallas_call(
        body,
        grid=(t // tm,),
        in_specs=[pl.BlockSpec((tm, D_MODEL), lambda i: (i, 0)),
                  _resident((D_MODEL, 2 * D_MODEL)), _resident((1, 2 * D_MODEL))],
        out_specs=pl.BlockSpec((tm, D_MODEL), lambda i: (i, 0)),
        out_shape=jax.ShapeDtypeStruct((t, D_MODEL), F32),
        compiler_params=_cparams("parallel"),
        name="conv_glu",
    )(x, w_in, b_in.reshape(1, 2 * D_MODEL))


def _conv_out_body(u_ref, up_ref, un_ref, x_ref, dw_ref, dwb_ref, ng_ref, nb_ref, w_ref, bo_ref, g_ref, b_ref,
                   y_ref, buf, act, *, ts):
    i = pl.program_id(1)
    halo = CONV_HALO
    pad = CONV_WIDTH // 2
    for s in range(D_MODEL // CONV_LANES):
        cs = slice(s * CONV_LANES, (s + 1) * CONV_LANES)
        buf[s, 0:halo, :] = jnp.where(i > 0, up_ref[0, :, cs], 0.0)
        buf[s, halo:halo + ts, :] = u_ref[0, :, cs]
        buf[s, halo + ts:halo + ts + halo, :] = jnp.where(i < pl.num_programs(1) - 1, un_ref[0, :, cs], 0.0)
    dwb, ng, nb = dwb_ref[...], ng_ref[...], nb_ref[...]
    n_tiles = ts // CONV_ROWS

    def conv_tile(t):
        base = pl.multiple_of(t * CONV_ROWS, CONV_ROWS)
        n_groups = CONV_ROWS // SUBLANES
        cols = []
        for s in range(D_MODEL // CONV_LANES):
            lanes = slice(s * CONV_LANES, (s + 1) * CONV_LANES)
            accs = [jnp.zeros((SUBLANES, CONV_LANES), F32) for _ in range(n_groups)]
            for o in range(halo - pad, CONV_ROWS + halo + pad - SUBLANES + 1):
                data = buf[s, pl.ds(base + o, SUBLANES, stride=1), :]
                for g in range(n_groups):
                    j = o - (halo - pad) - SUBLANES * g
                    if 0 <= j < CONV_WIDTH:
                        accs[g] = accs[g] + dw_ref[j:j + 1, lanes] * data
            cols.append(jnp.concatenate(accs, axis=0))
        c = _layer_norm(jnp.concatenate(cols, axis=1) + dwb, ng, nb)
        act[t % 2] = (c * jax.nn.sigmoid(c)).astype(BF16)

    def out_tile(t):
        rows = pl.ds(pl.multiple_of(t * CONV_ROWS, CONV_ROWS), CONV_ROWS)
        mix = _dot(act[t % 2], w_ref[...]) + bo_ref[...]
        y_ref[0, rows, :] = _layer_norm(ALPHA * x_ref[0, rows, :] + mix, g_ref[...], b_ref[...])

    conv_tile(0)

    def pipelined(t, carry):
        conv_tile(t)
        out_tile(t - 1)
        return carry

    lax.fori_loop(1, n_tiles, pipelined, 0)
    out_tile(n_tiles - 1)


def _conv_out(u, x, dw, dw_b, n_g, n_b, w_out, b_out, ln_g, ln_b, ts):
    bsz, seq, d = x.shape
    hb = ts // CONV_HALO
    last_halo = seq // CONV_HALO - 1

    def vec():
        return _resident((1, d))

    body = functools.partial(_conv_out_body, ts=ts)
    return pl.pallas_call(
        body,
        grid=(bsz, seq // ts),
        in_specs=[pl.BlockSpec((1, ts, d), lambda b, i: (b, i, 0)),
                  pl.BlockSpec((1, CONV_HALO, d), lambda b, i: (b, jnp.maximum(i * hb - 1, 0), 0)),
                  pl.BlockSpec((1, CONV_HALO, d), lambda b, i: (b, jnp.minimum((i + 1) * hb, last_halo), 0)),
                  pl.BlockSpec((1, ts, d), lambda b, i: (b, i, 0)),
                  _resident((CONV_WIDTH, d)),
                  vec(), vec(), vec(),
                  _resident((d, d)),
                  vec(), vec(), vec()],
        out_specs=pl.BlockSpec((1, ts, d), lambda b, i: (b, i, 0)),
        out_shape=jax.ShapeDtypeStruct((bsz, seq, d), F32),
        scratch_shapes=[pltpu.VMEM((d // CONV_LANES, ts + 2 * CONV_HALO, CONV_LANES), F32),
                        pltpu.VMEM((2, CONV_ROWS, d), BF16)],
        compiler_params=_cparams("parallel", "arbitrary"),
        name="conv_out_ln",
    )(u, u, u, x, dw, dw_b.reshape(1, d), n_g.reshape(1, d), n_b.reshape(1, d), w_out,
      b_out.reshape(1, d), ln_g.reshape(1, d), ln_b.reshape(1, d))


def _trunk(x, rel_bias, hgrn_lb, w_in_ab, hgrn_norm, w_out_ab, w_in_c, b_in_c, dw_c, dw_b_c, cnorm_g, cnorm_b,
           w_out_c, b_out_c, ln_mix_g, ln_mix_b, mlp_w1, mlp_w2, ln_ffn_g, ln_ffn_b):
    bsz, seq, d = x.shape
    t = bsz * seq
    x2 = x.reshape(t, d)
    for l in range(DEPTH):
        i = l // 2
        w1, w2 = mlp_w1[l].astype(BF16), mlp_w2[l].astype(BF16)
        if l % 2 == 0:
            wa = w_in_ab[i][:, :5 * A_WIDTH].reshape(d, 5, A_HEADS, A_HEAD_DIM)
            wa = wa.transpose(0, 2, 1, 3).reshape(d, 5 * A_WIDTH).astype(BF16)
            wb = w_in_ab[i][:, 5 * A_WIDTH:].astype(BF16)
            chunk = min(HGRN_CHUNK, seq)
            hg, hb = _in_proj(x2, wa, wb, hgrn_lb, hgrn_norm[i], min(IN_PROJ_ROWS, seq), chunk, l)
            oa = _hgrn_mixer(hg.reshape(bsz, seq, HG_PLANES * A_WIDTH), chunk)
            hb3 = hb.reshape(bsz, seq, 3 * B_WIDTH)
            outs, lses = [], []
            for gi, (window, r) in enumerate(B_GROUPS):
                hs = slice(gi * B_HEADS_PER_GROUP, (gi + 1) * B_HEADS_PER_GROUP)
                o, lse = _dilated_group(hb3, rel_bias[:, hs], gi, window, r)
                outs.append(o.reshape(t, B_GROUP_WIDTH))
                lses.append(lse.reshape(t, B_GROUP_WIDTH))
            x2 = _ab_out(oa.reshape(t, A_WIDTH), outs, lses, x2, w_out_ab[i].astype(BF16),
                         ln_mix_g[l], ln_mix_b[l], min(AB_OUT_ROWS, t))
        else:
            u = _glu_proj(x2, w_in_c[i].astype(BF16), b_in_c[i], min(GLU_ROWS, t), GLU_COLS)
            x2 = _conv_out(u.reshape(bsz, seq, d), x2.reshape(bsz, seq, d), dw_c[i], dw_b_c[i], cnorm_g[i],
                           cnorm_b[i], w_out_c[i].astype(BF16), b_out_c[i], ln_mix_g[l], ln_mix_b[l],
                           min(CONV_BLOCK, seq)).reshape(t, d)
        x2 = _mlp(x2, w1, w2, ln_ffn_g[l], ln_ffn_b[l], min(MLP_ROWS, t), min(MLP_SUB_ROWS, t))
    return x2.reshape(bsz, seq, d)


def kernel(x_prompt, x_sample, rel_bias, hgrn_lb, w_in_ab, hgrn_norm, w_out_ab, w_in_c, b_in_c, dw_c, dw_b_c,
           cnorm_g, cnorm_b, w_out_c, b_out_c, ln_mix_g, ln_mix_b, mlp_w1, mlp_w2, ln_ffn_g, ln_ffn_b):
    params = (rel_bias, hgrn_lb, w_in_ab, hgrn_norm, w_out_ab, w_in_c, b_in_c, dw_c, dw_b_c, cnorm_g, cnorm_b,
              w_out_c, b_out_c, ln_mix_g, ln_mix_b, mlp_w1, mlp_w2, ln_ffn_g, ln_ffn_b)
    return (_trunk(x_prompt, *params), _trunk(x_sample, *params))
```
